```python
import math
import jax, jax.numpy as jnp
from jax import lax
import numpy as np

D_MODEL = 2048
BATCH = 32
SEQ = 256
DEPTH = 4
DEC_BATCH = 4
DEC_SEQ = 1024
PAST_LEN = 256

GRID_W = 64
N_EVEN = (DEPTH + 1) // 2
N_ODD = DEPTH // 2
S5_WIDTH = D_MODEL // 2
S5_GROUP = 16
S5_GROUPS = S5_WIDTH // S5_GROUP
S5_STATE = 64
LRU_WIDTH = D_MODEL // 2
LRU_HEADS = 16
LRU_HEAD_DIM = LRU_WIDTH // LRU_HEADS
LRU_C = 8.0
CONV_W = 4
CONV_PAD_L = CONV_W // 2
CONV_PAD_R = CONV_W - 1 - CONV_PAD_L
EVEN_IN = S5_WIDTH + 2 * LRU_WIDTH
EVEN_OUT = S5_WIDTH + LRU_WIDTH
N_HEADS = 16
HEAD_DIM = D_MODEL // N_HEADS
WIN_ROWS_MAX = 8
WIN_COLS = 16
Q_COL_BLOCK = WIN_COLS
KEY_COL_BLOCK = 2 * WIN_COLS
N_COL_BLOCKS = GRID_W // Q_COL_BLOCK
CTX_Q_BLOCK = 128
N_EXPERTS = 16
N_GROUPS = 4
EXPERTS_PER_GROUP = N_EXPERTS // N_GROUPS
TOP_K = 2
D_EXPERT = 1024
NORM_EPS = 1e-6
NEG_INF = -1e30

kernel_name = 'hybrid_s5_rglru_natten_moe_diffusion_step'


def rmsnorm(x, g):
    xf = x.astype(jnp.float32)
    y = xf * lax.rsqrt(jnp.mean(xf * xf, axis=-1, keepdims=True) + NORM_EPS)
    return (y * g.astype(jnp.float32)).astype(x.dtype)


def modulate(x, g, shift, scale):
    return (rmsnorm(x, g) * (1.0 + scale) + shift).astype(x.dtype)


def _lin_combine(e1, e2):
    a1, b1 = e1
    a2, b2 = e2
    return a1 * a2, a2 * b1 + b2


def linear_scan(a, b, h0):
    a_cum, b_cum = lax.associative_scan(_lin_combine, (a, b), axis=1)
    return a_cum * h0[:, None] + b_cum


def s5_direction(u_g, h0, lam_re, lam_im, log_dt, b_re, b_im, c_re, c_im):
    f32 = jnp.float32
    lam = lax.complex(lam_re.astype(f32), lam_im.astype(f32))
    a_bar = jnp.exp(lam * jnp.exp(log_dt.astype(f32))[:, None])
    b_bar = ((a_bar - 1.0) / lam)[..., None] * lax.complex(b_re.astype(f32), b_im.astype(f32))
    bu = jnp.einsum('blgi,gpi->blgp', u_g.astype(jnp.complex64), b_bar)
    h = linear_scan(jnp.broadcast_to(a_bar, bu.shape), bu, h0)
    y = jnp.real(jnp.einsum('blgp,gip->blgi', h, lax.complex(c_re.astype(f32), c_im.astype(f32))))
    return y, h[:, -1]


def s5_mixer(u, h0, lam_re, lam_im, log_dt, b_re, b_im, c_re, c_im, d, w_glu, b_glu):
    f32 = jnp.float32
    bsz, L, _ = u.shape
    uf = u.astype(f32)
    u_g = uf.reshape(bsz, L, S5_GROUPS, S5_GROUP)
    y_f, hf_last = s5_direction(u_g, h0[:, 0], lam_re[0], lam_im[0], log_dt[0], b_re[0], b_im[0], c_re[0], c_im[0])
    y_b, hb_last = s5_direction(jnp.flip(u_g, 1), h0[:, 1], lam_re[1], lam_im[1], log_dt[1], b_re[1], b_im[1], c_re[1], c_im[1])
    y = (y_f + jnp.flip(y_b, 1)).reshape(bsz, L, S5_WIDTH) + d.astype(f32) * uf
    z = jax.nn.gelu(y)
    out = z * jax.nn.sigmoid(z @ w_glu.astype(f32) + b_glu.astype(f32))
    return out, jnp.stack([hf_last, hb_last], axis=1)


def rglru_direction(xc, h0, w_a, b_a, w_x, b_x, lam):
    f32 = jnp.float32
    bsz, L, _ = xc.shape
    xh = xc.reshape(bsz, L, LRU_HEADS, LRU_HEAD_DIM)
    r = jax.nn.sigmoid(jnp.einsum('blhi,hij->blhj', xh, w_a.astype(f32)).reshape(bsz, L, LRU_WIDTH) + b_a.astype(f32))
    i = jax.nn.sigmoid(jnp.einsum('blhi,hij->blhj', xh, w_x.astype(f32)).reshape(bsz, L, LRU_WIDTH) + b_x.astype(f32))
    log_a = -LRU_C * r * jax.nn.softplus(-lam.astype(f32))
    a = jnp.exp(log_a)
    b = jnp.sqrt(-jnp.expm1(2.0 * log_a)) * (i * xc)
    return linear_scan(a, b, h0)


def rglru_mixer(gate, xr, h0, conv_w, conv_b, w_a, b_a, w_x, b_x, lam):
    f32 = jnp.float32
    xc = lax.conv_general_dilated(xr.astype(f32), conv_w.astype(f32)[:, None, :], (1,),
                                  [(CONV_PAD_L, CONV_PAD_R)],
                                  dimension_numbers=('NWC', 'WIO', 'NWC'),
                                  feature_group_count=LRU_WIDTH) + conv_b.astype(f32)
    h_f = rglru_direction(xc, h0[:, 0], w_a[0], b_a[0], w_x[0], b_x[0], lam[0])
    h_b = rglru_direction(jnp.flip(xc, 1), h0[:, 1], w_a[1], b_a[1], w_x[1], b_x[1], lam[1])
    out = jax.nn.gelu(gate.astype(f32)) * (h_f + jnp.flip(h_b, 1))
    return out, jnp.stack([h_f[:, -1], h_b[:, -1]], axis=1)


def even_mixer(h, w_in, w_out, s5p, lrup, s5_h0, lru_h0):
    proj = h @ w_in
    y_s5, s5_fin = s5_mixer(proj[..., :S5_WIDTH], s5_h0, **s5p)
    y_lru, lru_fin = rglru_mixer(proj[..., S5_WIDTH:S5_WIDTH + LRU_WIDTH],
                                 proj[..., S5_WIDTH + LRU_WIDTH:], lru_h0, **lrup)
    out = jnp.concatenate([y_s5, y_lru], axis=-1).astype(h.dtype) @ w_out
    return out, s5_fin, lru_fin


def attn_qkv(h, w_qkv):
    bsz, L, _ = h.shape
    q, k, v = jnp.split(h @ w_qkv, 3, axis=-1)
    return (q.reshape(bsz, L, N_HEADS, HEAD_DIM), k.reshape(bsz, L, N_HEADS, HEAD_DIM),
            v.reshape(bsz, L, N_HEADS, HEAD_DIM))


def ctx_attention(q, k, v):
    bsz, m, _, _ = q.shape
    nblk = m // CTX_Q_BLOCK
    qb = jnp.moveaxis(q.reshape(bsz, nblk, CTX_Q_BLOCK, N_HEADS, HEAD_DIM), 1, 0)
    scale = HEAD_DIM ** -0.5

    def block(qi):
        s = jnp.einsum('bqhd,bmhd->bhqm', qi, k).astype(jnp.float32) * scale
        p = jax.nn.softmax(s, axis=-1).astype(v.dtype)
        return jnp.einsum('bhqm,bmhd->bqhd', p, v)

    o = lax.map(block, qb)
    return jnp.moveaxis(o, 0, 1).reshape(bsz, m, D_MODEL)


def neighbourhood_attention(q, k, v, k_ctx, v_ctx, rpb):
    f32 = jnp.float32
    bsz, n, _, _ = q.shape
    rows = n // GRID_W
    kr = min(WIN_ROWS_MAX, rows)
    r_idx = np.arange(rows)
    row_start = np.clip(r_idx - kr // 2, 0, rows - kr)
    row_off = row_start[:, None] + np.arange(kr)[None, :] - r_idx[:, None] + WIN_ROWS_MAX - 1
    qcol = np.arange(GRID_W).reshape(N_COL_BLOCKS, Q_COL_BLOCK)
    col_start = np.clip(qcol - WIN_COLS // 2, 0, GRID_W - WIN_COLS)
    kb_start = np.clip(np.arange(N_COL_BLOCKS) * Q_COL_BLOCK - WIN_COLS // 2, 0, GRID_W - KEY_COL_BLOCK)
    kcol = kb_start[:, None] + np.arange(KEY_COL_BLOCK)[None, :]
    col_valid = ((kcol[:, None, :] >= col_start[:, :, None]) &
                 (kcol[:, None, :] < col_start[:, :, None] + WIN_COLS))
    col_off = np.clip(kcol[:, None, :] - qcol[:, :, None] + WIN_COLS - 1, 0, 2 * WIN_COLS - 2)
    bias_cm = jnp.where(col_valid, rpb.astype(f32)[:, :, col_off], NEG_INF)
    bias_rows = jnp.transpose(bias_cm[:, row_off], (1, 0, 3, 4, 2, 5))
    q_rows = jnp.moveaxis(q.reshape(bsz, rows, N_COL_BLOCKS, Q_COL_BLOCK, N_HEADS, HEAD_DIM), 1, 0)
    k_grid = k.reshape(bsz, rows, GRID_W, N_HEADS, HEAD_DIM)
    v_grid = v.reshape(bsz, rows, GRID_W, N_HEADS, HEAD_DIM)
    scale = HEAD_DIM ** -0.5
    n_loc = kr * KEY_COL_BLOCK

    def row_block(args):
        q_r, rs, bias_r = args
        k_blk = lax.dynamic_slice_in_dim(k_grid, rs, kr, axis=1)[:, :, kcol]
        v_blk = lax.dynamic_slice_in_dim(v_grid, rs, kr, axis=1)[:, :, kcol]
        s_loc = jnp.einsum('bcqhd,bkcjhd->bhcqkj', q_r, k_blk).astype(f32) * scale + bias_r[None]
        s_ctx = jnp.einsum('bcqhd,bmhd->bhcqm', q_r, k_ctx).astype(f32) * scale
        s = jnp.concatenate([s_loc.reshape(bsz, N_HEADS, N_COL_BLOCKS, Q_COL_BLOCK, n_loc), s_ctx], axis=-1)
        p = jax.nn.softmax(s, axis=-1).astype(v.dtype)
        p_loc = p[..., :n_loc].reshape(bsz, N_HEADS, N_COL_BLOCKS, Q_COL_BLOCK, kr, KEY_COL_BLOCK)
        o = (jnp.einsum('bhcqkj,bkcjhd->bcqhd', p_loc, v_blk) +
             jnp.einsum('bhcqm,bmhd->bcqhd', p[..., n_loc:], v_ctx))
        return o.reshape(bsz, GRID_W, N_HEADS, HEAD_DIM)

    o = lax.map(row_block, (q_rows, jnp.asarray(row_start, jnp.int32), bias_rows))
    return jnp.moveaxis(o, 0, 1).reshape(bsz, n, D_MODEL)


def moe_ffn(h, w_router, b_router, w_gu, w_dn):
    f32 = jnp.float32
    shp = h.shape
    x = h.reshape(-1, D_MODEL)
    scores = jax.nn.sigmoid((x @ w_router).astype(f32))
    sel = (scores + b_router.astype(f32)).reshape(-1, N_GROUPS, EXPERTS_PER_GROUP)
    group_score = jnp.sum(lax.top_k(sel, TOP_K)[0], axis=-1)
    g = jnp.argmax(group_score, axis=-1)
    sel_in = jnp.take_along_axis(sel, g[:, None, None], axis=1)[:, 0]
    _, local = lax.top_k(sel_in, TOP_K)
    eid = g[:, None] * EXPERTS_PER_GROUP + local
    wk = jnp.take_along_axis(scores, eid, axis=1)
    wk = wk / jnp.sum(wk, axis=-1, keepdims=True)
    gates = jnp.sum(jax.nn.one_hot(eid, N_EXPERTS, dtype=f32) * wk[..., None], axis=1)
    y = jnp.zeros(x.shape, f32)
    for e in range(N_EXPERTS):
        a, b = jnp.split(x @ w_gu[e], 2, axis=-1)
        y = y + gates[:, e:e + 1] * ((jax.nn.silu(a) * b) @ w_dn[e])
    return y.astype(h.dtype).reshape(shp)


def setup_inputs(seed: int = 0) -> dict:
    key = jax.random.key(seed)
    ks = iter(jax.random.split(key, 64))
    f32 = jnp.float32

    def nrm(shape, scale=1.0):
        return jax.random.normal(next(ks), shape, f32) * scale

    def gain(shape):
        return 1.0 + nrm(shape, 0.01)

    G, P = S5_GROUPS, S5_STATE
    lam_im_base = jnp.pi * jnp.arange(P, dtype=f32)
    u_lru = jax.random.uniform(next(ks), (N_EVEN, 2, LRU_WIDTH), f32, minval=0.9, maxval=0.999)
    a0 = u_lru ** (1.0 / LRU_C)
    return {
        'x_prompt': nrm((BATCH, SEQ, D_MODEL)),
        'x_sample': nrm((DEC_BATCH, DEC_SEQ, D_MODEL)),
        'state_s5_re': nrm((DEC_BATCH, N_EVEN, 2, G, P), 0.1),
        'state_s5_im': nrm((DEC_BATCH, N_EVEN, 2, G, P), 0.1),
        'state_lru': nrm((DEC_BATCH, N_EVEN, 2, LRU_WIDTH), 0.5),
        'cache_attn_k': nrm((DEC_BATCH, N_ODD, PAST_LEN, N_HEADS, HEAD_DIM)),
        'cache_attn_v': nrm((DEC_BATCH, N_ODD, PAST_LEN, N_HEADS, HEAD_DIM)),
        'c': nrm((DEC_BATCH, D_MODEL)),
        'c_ctx': nrm((D_MODEL,)),
        'w_ada': nrm((DEPTH, D_MODEL, 6 * D_MODEL), 0.5 * D_MODEL ** -0.5),
        'b_ada': nrm((DEPTH, 6 * D_MODEL), 0.01),
        'norm1_g': gain((DEPTH, D_MODEL)),
        'norm2_g': gain((DEPTH, D_MODEL)),
        'final_norm_g': gain((D_MODEL,)),
        'w_in_even': nrm((N_EVEN, D_MODEL, EVEN_IN), D_MODEL ** -0.5),
        'w_out_even': nrm((N_EVEN, EVEN_OUT, D_MODEL), EVEN_OUT ** -0.5),
        's5_lam_re': -0.5 + nrm((N_EVEN, 2, G, P), 0.01),
        's5_lam_im': lam_im_base + nrm((N_EVEN, 2, G, P), 0.01),
        's5_log_dt': jax.random.uniform(next(ks), (N_EVEN, 2, G), f32, minval=math.log(1e-3), maxval=math.log(1e-1)),
        's5_b_re': nrm((N_EVEN, 2, G, P, S5_GROUP), (2 * S5_GROUP) ** -0.5),
        's5_b_im': nrm((N_EVEN, 2, G, P, S5_GROUP), (2 * S5_GROUP) ** -0.5),
        's5_c_re': nrm((N_EVEN, 2, G, S5_GROUP, P), S5_STATE ** -0.5),
        's5_c_im': nrm((N_EVEN, 2, G, S5_GROUP, P), S5_STATE ** -0.5),
        's5_d': nrm((N_EVEN, S5_WIDTH)),
        's5_w_glu': nrm((N_EVEN, S5_WIDTH, S5_WIDTH), S5_WIDTH ** -0.5),
        's5_b_glu': nrm((N_EVEN, S5_WIDTH), 0.01),
        'lru_conv_w': nrm((N_EVEN, CONV_W, LRU_WIDTH), CONV_W ** -0.5),
        'lru_conv_b': nrm((N_EVEN, LRU_WIDTH), 0.01),
        'lru_w_a': nrm((N_EVEN, 2, LRU_HEADS, LRU_HEAD_DIM, LRU_HEAD_DIM), LRU_HEAD_DIM ** -0.5),
        'lru_b_a': nrm((N_EVEN, 2, LRU_WIDTH), 0.01),
        'lru_w_x': nrm((N_EVEN, 2, LRU_HEADS, LRU_HEAD_DIM, LRU_HEAD_DIM), LRU_HEAD_DIM ** -0.5),
        'lru_b_x': nrm((N_EVEN, 2, LRU_WIDTH), 0.01),
        'lru_lam': jnp.log(a0) - jnp.log1p(-a0),
        'w_qkv': nrm((N_ODD, D_MODEL, 3 * D_MODEL), D_MODEL ** -0.5),
        'w_o': nrm((N_ODD, D_MODEL, D_MODEL), D_MODEL ** -0.5),
        'rpb': nrm((N_ODD, N_HEADS, 2 * WIN_ROWS_MAX - 1, 2 * WIN_COLS - 1), 0.1),
        'w_router': nrm((D_MODEL, N_EXPERTS), D_MODEL ** -0.5),
        'b_router': nrm((N_EXPERTS,), 0.01),
        'w_gate_up': nrm((DEPTH, N_EXPERTS, D_MODEL, 2 * D_EXPERT), D_MODEL ** -0.5),
        'w_down': nrm((DEPTH, N_EXPERTS, D_EXPERT, D_MODEL), D_EXPERT ** -0.5),
    }


def reference(x_prompt, x_sample, state_s5_re, state_s5_im, state_lru, cache_attn_k, cache_attn_v,
              c, c_ctx, w_ada, b_ada, norm1_g, norm2_g, final_norm_g,
              w_in_even, w_out_even, s5_lam_re, s5_lam_im, s5_log_dt, s5_b_re, s5_b_im,
              s5_c_re, s5_c_im, s5_d, s5_w_glu, s5_b_glu,
              lru_conv_w, lru_conv_b, lru_w_a, lru_b_a, lru_w_x, lru_b_x, lru_lam,
              w_qkv, w_o, rpb, w_router, b_router, w_gate_up, w_down):
    f32 = jnp.float32
    mod_ctx = jnp.einsum('d,ldk->lk', jax.nn.silu(c_ctx), w_ada) + b_ada
    mod_lat = jnp.einsum('bd,ldk->lbk', jax.nn.silu(c), w_ada) + b_ada[:, None]
    bp = x_prompt.shape[0]
    xp, xs = x_prompt, x_sample
    s5_re_list, s5_im_list, lru_list, k_list, v_list = [], [], [], [], []
    for l in range(DEPTH):
        sh1c, sc1c, g1c, sh2c, sc2c, g2c = jnp.split(mod_ctx[l], 6)
        sh1s, sc1s, g1s, sh2s, sc2s, g2s = [m[:, None] for m in jnp.split(mod_lat[l], 6, axis=-1)]
        hp = modulate(xp, norm1_g[l], sh1c, sc1c)
        hs = modulate(xs, norm1_g[l], sh1s, sc1s)
        j = l // 2
        if l % 2 == 0:
            s5p = dict(lam_re=s5_lam_re[j], lam_im=s5_lam_im[j], log_dt=s5_log_dt[j],
                       b_re=s5_b_re[j], b_im=s5_b_im[j], c_re=s5_c_re[j], c_im=s5_c_im[j],
                       d=s5_d[j], w_glu=s5_w_glu[j], b_glu=s5_b_glu[j])
            lrup = dict(conv_w=lru_conv_w[j], conv_b=lru_conv_b[j], w_a=lru_w_a[j], b_a=lru_b_a[j],
                        w_x=lru_w_x[j], b_x=lru_b_x[j], lam=lru_lam[j])
            zero_s5 = jnp.zeros((bp, 2, S5_GROUPS, S5_STATE), jnp.complex64)
            zero_lru = jnp.zeros((bp, 2, LRU_WIDTH), f32)
            op, s5_fin, lru_fin = even_mixer(hp, w_in_even[j], w_out_even[j], s5p, lrup, zero_s5, zero_lru)
            s5_re_list.append(jnp.real(s5_fin))
            s5_im_list.append(jnp.imag(s5_fin))
            lru_list.append(lru_fin)
            s5_h0 = lax.complex(state_s5_re[:, j].astype(f32), state_s5_im[:, j].astype(f32))
            os_, _, _ = even_mixer(hs, w_in_even[j], w_out_even[j], s5p, lrup, s5_h0,
                                   state_lru[:, j].astype(f32))
        else:
            qp, kp, vp = attn_qkv(hp, w_qkv[j])
            op = ctx_attention(qp, kp, vp) @ w_o[j]
            k_list.append(kp)
            v_list.append(vp)
            qs, ks_, vs = attn_qkv(hs, w_qkv[j])
            os_ = neighbourhood_attention(qs, ks_, vs, cache_attn_k[:, j].astype(qs.dtype),
                                          cache_attn_v[:, j].astype(vs.dtype), rpb[j]) @ w_o[j]
        xp = xp + g1c * op
        xs = xs + g1s * os_
        xp = xp + g2c * moe_ffn(modulate(xp, norm2_g[l], sh2c, sc2c), w_router, b_router, w_gate_up[l], w_down[l])
        xs = xs + g2s * moe_ffn(modulate(xs, norm2_g[l], sh2s, sc2s), w_router, b_router, w_gate_up[l], w_down[l])
    y_prompt = rmsnorm(xp, final_norm_g)
    y_sample = rmsnorm(xs, final_norm_g)
    new_state_s5_re = jnp.stack(s5_re_list, axis=1)
    new_state_s5_im = jnp.stack(s5_im_list, axis=1)
    new_state_lru = jnp.stack(lru_list, axis=1)
    new_cache_attn_k = jnp.stack(k_list, axis=1)
    new_cache_attn_v = jnp.stack(v_list, axis=1)
    return (y_prompt, y_sample, new_state_s5_re, new_state_s5_im, new_state_lru, new_cache_attn_k, new_cache_attn_v)
```

```python
import functools
import math

import numpy as np
import jax
import jax.numpy as jnp
from jax import lax
from jax.experimental import pallas as pl
from jax.experimental.pallas import tpu as pltpu

F32 = jnp.float32
BF16 = jnp.bfloat16

NORM_EPS = 1e-6
NEG_INF = -1e30
S5_GROUP = 16
S5_CHUNK = 16
LRU_C = 8.0
LRU_HEAD_BLOCK = 256
N_EXPERT_GROUPS = 4
WIN_ROWS_MAX = 8
WIN_COLS = 16
GRID_W = 64
N_MOD = 6
MOD_ROWS = 8
VMEM_LIMIT = 52 * 1024 * 1024
ROW_TILE = 256
MM_TM = 512
MM_TN = 1024
MOE_TM = 256
MOE_TN = 512


def _cparams(*sem):
    return pltpu.CompilerParams(dimension_semantics=sem, vmem_limit_bytes=VMEM_LIMIT)


def _gelu(x):
    return 0.5 * x * (1.0 + jnp.tanh(math.sqrt(2.0 / math.pi) * (x + 0.044715 * (x * x * x))))


class _Layout:
    def __init__(self, n_ctx_rows, dec_batch, dec_seq):
        self.n_ctx = n_ctx_rows
        self.dec_batch = dec_batch
        self.dec_seq = dec_seq
        self.total = n_ctx_rows + dec_batch * dec_seq

    def mod_row(self, i, tm):
        nct = self.n_ctx // tm
        per = self.dec_seq // tm
        return jnp.where(i < nct, self.dec_batch, (i - nct) // per)


def _mod_index(layer, which, row):
    return (layer * MOD_ROWS + row) * N_MOD + which


def _ada_kernel(c_ref, w_ref, b_ref, o_ref):
    c = c_ref[...]
    s = (c * jax.nn.sigmoid(c)).astype(BF16)
    o_ref[...] = jnp.dot(s, w_ref[...].astype(BF16), preferred_element_type=F32) + b_ref[...]


def _ada_project(cvec, w_ada, b_ada):
    depth, d, n = w_ada.shape
    tn = MM_TN
    return pl.pallas_call(
        _ada_kernel,
        out_shape=jax.ShapeDtypeStruct((depth, MOD_ROWS, n), F32),
        grid=(depth, n // tn),
        in_specs=[
            pl.BlockSpec((MOD_ROWS, d), lambda l, j: (0, 0)),
            pl.BlockSpec((None, d, tn), lambda l, j: (l, 0, j)),
            pl.BlockSpec((None, 1, tn), lambda l, j: (l, 0, j)),
        ],
        out_specs=pl.BlockSpec((None, MOD_ROWS, tn), lambda l, j: (l, 0, j)),
        compiler_params=_cparams("arbitrary", "arbitrary"),
        name="ada_project",
    )(cvec, w_ada, b_ada.reshape(depth, 1, n))


def _rms(x, g):
    ms = jnp.mean(x * x, axis=-1, keepdims=True)
    return x * lax.rsqrt(ms + NORM_EPS) * g


def _norm_mod_kernel(x_ref, g_ref, sh_ref, sc_ref, o_ref):
    y = _rms(x_ref[...], g_ref[...])
    o_ref[...] = (y * (1.0 + sc_ref[...]) + sh_ref[...]).astype(o_ref.dtype)


def _norm_mod(x, g_row, mods3, layer, which_shift, lay, out_dtype):
    t, d = x.shape
    tm = ROW_TILE
    row = lambda i: lay.mod_row(i, tm)
    return pl.pallas_call(
        _norm_mod_kernel,
        out_shape=jax.ShapeDtypeStruct((t, d), out_dtype),
        grid=(t // tm,),
        in_specs=[
            pl.BlockSpec((tm, d), lambda i: (i, 0)),
            pl.BlockSpec((1, d), lambda i: (0, 0)),
            pl.BlockSpec((None, 1, d), lambda i: (_mod_index(layer, which_shift, row(i)), 0, 0)),
            pl.BlockSpec((None, 1, d), lambda i: (_mod_index(layer, which_shift + 1, row(i)), 0, 0)),
        ],
        out_specs=pl.BlockSpec((tm, d), lambda i: (i, 0)),
        compiler_params=_cparams("arbitrary"),
        name="norm_mod",
    )(x, g_row, mods3, mods3)


def _top2_of4(a):
    m1 = jnp.maximum(jnp.maximum(a[0], a[1]), jnp.maximum(a[2], a[3]))
    i1 = jnp.where(a[0] == m1, 0, jnp.where(a[1] == m1, 1, jnp.where(a[2] == m1, 2, 3)))
    b = [jnp.where(i1 == k, -jnp.inf, a[k]) for k in range(4)]
    m2 = jnp.maximum(jnp.maximum(b[0], b[1]), jnp.maximum(b[2], b[3]))
    i2 = jnp.where(b[0] == m2, 0, jnp.where(b[1] == m2, 1, jnp.where(b[2] == m2, 2, 3)))
    return m1 + m2, i1, i2


def _norm_mod_route_kernel(x_ref, g_ref, sh_ref, sc_ref, wr_ref, br_ref, o_ref, r_ref):
    y = _rms(x_ref[...], g_ref[...])
    h = y * (1.0 + sc_ref[...]) + sh_ref[...]
    o_ref[...] = h.astype(o_ref.dtype)
    logits = lax.dot_general(wr_ref[...], h, (((1,), (1,)), ((), ())),
                             precision=lax.Precision.HIGHEST, preferred_element_type=F32)
    scores = jax.nn.sigmoid(logits)
    sel = scores + br_ref[...]
    n_e = scores.shape[0]
    per = n_e // N_EXPERT_GROUPS
    sel_rows = [sel[e:e + 1, :] for e in range(n_e)]
    score_rows = [scores[e:e + 1, :] for e in range(n_e)]
    gs, i1s, i2s = [], [], []
    for gi in range(N_EXPERT_GROUPS):
        s, i1, i2 = _top2_of4(sel_rows[gi * per:(gi + 1) * per])
        gs.append(s)
        i1s.append(i1)
        i2s.append(i2)
    gmax = jnp.maximum(jnp.maximum(gs[0], gs[1]), jnp.maximum(gs[2], gs[3]))
    gsel = jnp.where(gs[0] == gmax, 0, jnp.where(gs[1] == gmax, 1, jnp.where(gs[2] == gmax, 2, 3)))
    l1 = jnp.where(gsel == 0, i1s[0], jnp.where(gsel == 1, i1s[1], jnp.where(gsel == 2, i1s[2], i1s[3])))
    l2 = jnp.where(gsel == 0, i2s[0], jnp.where(gsel == 1, i2s[1], jnp.where(gsel == 2, i2s[2], i2s[3])))
    e1 = gsel * per + l1
    e2 = gsel * per + l2
    w1 = jnp.zeros_like(gmax)
    w2 = jnp.zeros_like(gmax)
    for e in range(n_e):
        w1 = jnp.where(e1 == e, score_rows[e], w1)
        w2 = jnp.where(e2 == e, score_rows[e], w2)
    wsum = w1 + w2
    zero = jnp.zeros_like(gmax)
    r_ref[...] = jnp.concatenate(
        [e1.astype(F32), e2.astype(F32), w1 / wsum, w2 / wsum, zero, zero, zero, zero], axis=0)


def _norm_mod_route(x, g_row, mods3, layer, lay, w_router_t, b_router_col):
    t, d = x.shape
    tm = ROW_TILE
    n_e = w_router_t.shape[0]
    row = lambda i: lay.mod_row(i, tm)
    return pl.pallas_call(
        _norm_mod_route_kernel,
        out_shape=(jax.ShapeDtypeStruct((t, d), F32), jax.ShapeDtypeStruct((8, t), F32)),
        grid=(t // tm,),
        in_specs=[
            pl.BlockSpec((tm, d), lambda i: (i, 0)),
            pl.BlockSpec((1, d), lambda i: (0, 0)),
            pl.BlockSpec((None, 1, d), lambda i: (_mod_index(layer, 3, row(i)), 0, 0)),
            pl.BlockSpec((None, 1, d), lambda i: (_mod_index(layer, 4, row(i)), 0, 0)),
            pl.BlockSpec((n_e, d), lambda i: (0, 0)),
            pl.BlockSpec((n_e, 1), lambda i: (0, 0)),
        ],
        out_specs=(pl.BlockSpec((tm, d), lambda i: (i, 0)), pl.BlockSpec((8, tm), lambda i: (0, i))),
        compiler_params=_cparams("arbitrary"),
        name="norm_mod_route",
    )(x, g_row, mods3, mods3, w_router_t, b_router_col)


def _final_norm_kernel(x_ref, g_ref, o_ref):
    o_ref[...] = _rms(x_ref[...], g_ref[...])


def _final_norm(x, g_row):
    t, d = x.shape
    tm = ROW_TILE
    return pl.pallas_call(
        _final_norm_kernel,
        out_shape=jax.ShapeDtypeStruct((t, d), F32),
        grid=(t // tm,),
        in_specs=[pl.BlockSpec((tm, d), lambda i: (i, 0)), pl.BlockSpec((1, d), lambda i: (0, 0))],
        out_specs=pl.BlockSpec((tm, d), lambda i: (i, 0)),
        compiler_params=_cparams("arbitrary"),
        name="final_norm",
    )(x, g_row)


def _mm_kernel(*refs, n_a, resid):
    a_refs = refs[:n_a]
    w_ref = refs[n_a]
    pos = n_a + 1
    if resid:
        x_ref, g_ref = refs[pos], refs[pos + 1]
        pos += 2
    o_ref, wb_ref = refs[pos], refs[pos + 1]

    @pl.when(pl.program_id(1) == 0)
    def _():
        wb_ref[...] = w_ref[...].astype(BF16)

    acc = None
    k0 = 0
    for a_ref in a_refs:
        ka = a_ref.shape[1]
        part = jnp.dot(a_ref[...], wb_ref[k0:k0 + ka, :], preferred_element_type=F32)
        acc = part if acc is None else acc + part
        k0 += ka
    if resid:
        o_ref[...] = x_ref[...] + g_ref[...] * acc
    else:
        o_ref[...] = acc.astype(o_ref.dtype)


def _matmul(a_list, w, out_dtype, resid=None):
    m = a_list[0].shape[0]
    k, n = w.shape
    tm, tn = MM_TM, min(MM_TN, n)
    in_specs = [pl.BlockSpec((tm, a.shape[1]), lambda j, i: (i, 0)) for a in a_list]
    in_specs.append(pl.BlockSpec((k, tn), lambda j, i: (0, j)))
    args = list(a_list) + [w]
    if resid is not None:
        x, mods3, layer, which, lay = resid
        in_specs.append(pl.BlockSpec((tm, tn), lambda j, i: (i, j)))
        in_specs.append(pl.BlockSpec(
            (None, 1, tn), lambda j, i: (_mod_index(layer, which, lay.mod_row(i, tm)), 0, j)))
        args += [x, mods3]
    return pl.pallas_call(
        functools.partial(_mm_kernel, n_a=len(a_list), resid=resid is not None),
        out_shape=jax.ShapeDtypeStruct((m, n), out_dtype),
        grid=(n // tn, m // tm),
        in_specs=in_specs,
        out_specs=pl.BlockSpec((tm, tn), lambda j, i: (i, j)),
        scratch_shapes=[pltpu.VMEM((k, tn), BF16)],
        compiler_params=_cparams("arbitrary", "arbitrary"),
        name="matmul_resid" if resid is not None else "matmul",
    )(*args)


def _cmul(ar, ai, br, bi):
    return ar * br - ai * bi, ar * bi + ai * br


def _s5_matrices(lam_re, lam_im, log_dt, b_re, b_im, c_re, c_im):
    hi = lax.Precision.HIGHEST
    s = S5_CHUNK
    f = lambda z: z.astype(F32)
    lam_re, lam_im, log_dt, b_re, b_im, c_re, c_im = map(f, (lam_re, lam_im, log_dt, b_re, b_im, c_re, c_im))
    dt = jnp.exp(log_dt)[..., None]
    kk = jnp.arange(s + 1, dtype=F32)[:, None, None, None]
    mag = jnp.exp(kk * (lam_re * dt)[None])
    ph = kk * (lam_im * dt)[None]
    pw_re, pw_im = mag * jnp.cos(ph), mag * jnp.sin(ph)
    a_re, a_im = pw_re[1], pw_im[1]
    den = lam_re * lam_re + lam_im * lam_im
    q_re, q_im = _cmul(a_re - 1.0, a_im, lam_re / den, -lam_im / den)
    bb_re, bb_im = _cmul(q_re[..., None], q_im[..., None], b_re, b_im)
    cp_re, cp_im = _cmul(c_re[:, :, None], c_im[:, :, None],
                         jnp.moveaxis(pw_re, 0, 2)[:, :, :, None, :], jnp.moveaxis(pw_im, 0, 2)[:, :, :, None, :])
    kern = (jnp.einsum('dgtip,dgpj->dgtij', cp_re, bb_re, precision=hi)
            - jnp.einsum('dgtip,dgpj->dgtij', cp_im, bb_im, precision=hi))
    sp = np.arange(s)[:, None]
    so = np.arange(s)[None, :]
    lag_f = np.clip(so - sp, 0, s)
    lag_b = np.clip(sp - so, 0, s)
    kf = jnp.where(jnp.asarray(so >= sp)[None, :, :, None, None], kern[0][:, lag_f], 0.0)
    kb = jnp.where(jnp.asarray(sp >= so)[None, :, :, None, None], kern[1][:, lag_b], 0.0)
    g = kern.shape[1]
    n = s * S5_GROUP
    tmat = jnp.transpose(kf + kb, (0, 1, 4, 2, 3)).reshape(g, n, n)

    def e_mat(d, powers):
        pr = jnp.moveaxis(pw_re[powers, d], 0, 1)[:, :, :, None]
        pi = jnp.moveaxis(pw_im[powers, d], 0, 1)[:, :, :, None]
        er, ei = _cmul(pr, pi, bb_re[d][:, None], bb_im[d][:, None])
        er = jnp.transpose(er, (0, 1, 3, 2)).reshape(g, n, -1)
        ei = jnp.transpose(ei, (0, 1, 3, 2)).reshape(g, n, -1)
        return jnp.concatenate([er, ei], axis=-1)

    def c_mat(d, powers):
        cr = jnp.transpose(cp_re[d][:, powers], (0, 3, 1, 2)).reshape(g, -1, n)
        ci = jnp.transpose(cp_im[d][:, powers], (0, 3, 1, 2)).reshape(g, -1, n)
        return jnp.concatenate([cr, -ci], axis=1)

    e_f = e_mat(0, np.arange(s - 1, -1, -1))
    e_b = e_mat(1, np.arange(s))
    c_f = c_mat(0, np.arange(1, s + 1))
    c_b = c_mat(1, np.arange(s, 0, -1))
    dec_r = jnp.concatenate([pw_re[s], pw_re[s]], axis=-1)
    dec_i = jnp.concatenate([-pw_im[s], pw_im[s]], axis=-1)
    decay = jnp.stack([dec_r, dec_i], axis=2)[:, :, :, None, :]
    return tmat.astype(BF16), e_f.astype(BF16), e_b.astype(BF16), c_f.astype(BF16), c_b.astype(BF16), decay


def _s5_kernel(u_ref, t_ref, ef_ref, eb_ref, cf_ref, cb_ref, dec_ref, h0_ref,
               y_ref, hfin_ref, zf, zb, hsf, hsb, *, nb, nc):
    u = u_ref[...]
    zf[...] = jnp.dot(u, ef_ref[...], preferred_element_type=F32)
    zb[...] = jnp.dot(u, eb_ref[...], preferred_element_type=F32)
    half = zf.shape[1] // 2

    def carry(z_ref, hs_ref, d, order):
        ar = dec_ref[d, 0]
        ai = dec_ref[d, 1]
        h = h0_ref[d]
        for c in order:
            rows = pl.ds(c * nb, nb)
            hs_ref[rows, :] = h
            h = ar * h + ai * pltpu.roll(h, half, 1) + z_ref[rows, :]
        hfin_ref[d] = h

    carry(zf, hsf, 0, range(nc))
    carry(zb, hsb, 1, range(nc - 1, -1, -1))
    y = jnp.dot(u, t_ref[...], preferred_element_type=F32)
    y = y + jnp.dot(hsf[...].astype(BF16), cf_ref[...], preferred_element_type=F32)
    y = y + jnp.dot(hsb[...].astype(BF16), cb_ref[...], preferred_element_type=F32)
    y_ref[...] = y


def _s5_scan(u_tok, h0, mats, n_seq, seq_len):
    tmat, e_f, e_b, c_f, c_b, decay = mats
    g = tmat.shape[0]
    n = tmat.shape[1]
    p2 = e_f.shape[2]
    s = S5_CHUNK
    nc = seq_len // s
    nb = -(-n_seq // 8) * 8
    r = nc * nb
    u5 = u_tok.astype(BF16).reshape(n_seq, nc, s, g, S5_GROUP)
    up = jnp.transpose(u5, (3, 1, 0, 2, 4))
    up = jnp.pad(up, ((0, 0), (0, 0), (0, nb - n_seq), (0, 0), (0, 0))).reshape(g, r, n)
    h0p = jnp.pad(jnp.transpose(h0, (2, 1, 0, 3)), ((0, 0), (0, 0), (0, nb - n_seq), (0, 0)))
    wspec = lambda shape: pl.BlockSpec((None,) + shape, lambda gi: (gi, 0, 0))
    y, hfin = pl.pallas_call(
        functools.partial(_s5_kernel, nb=nb, nc=nc),
        out_shape=(jax.ShapeDtypeStruct((g, r, n), F32), jax.ShapeDtypeStruct((g, 2, nb, p2), F32)),
        grid=(g,),
        in_specs=[
            wspec((r, n)), wspec((n, n)), wspec((n, p2)), wspec((n, p2)), wspec((p2, n)), wspec((p2, n)),
            pl.BlockSpec((2, None, 2, 1, p2), lambda gi: (0, gi, 0, 0, 0)),
            pl.BlockSpec((None, 2, nb, p2), lambda gi: (gi, 0, 0, 0)),
        ],
        out_specs=(wspec((r, n)), pl.BlockSpec((None, 2, nb, p2), lambda gi: (gi, 0, 0, 0))),
        scratch_shapes=[pltpu.VMEM((r, p2), F32)] * 4,
        compiler_params=_cparams("arbitrary"),
        name="s5_scan",
    )(up, tmat, e_f, e_b, c_f, c_b, decay, h0p)
    y5 = y.reshape(g, nc, nb, s, S5_GROUP)[:, :, :n_seq]
    y_tok = jnp.transpose(y5, (2, 1, 3, 0, 4)).reshape(n_seq * seq_len, g * S5_GROUP)
    return y_tok, jnp.transpose(hfin[:, :, :n_seq], (2, 1, 0, 3))


def _s5_glu_kernel(y_ref, u_ref, d_ref, w_ref, b_ref, o_ref, wb_ref):
    @pl.when(pl.program_id(0) == 0)
    def _():
        wb_ref[...] = w_ref[...].astype(BF16)

    z = _gelu(y_ref[...] + d_ref[...] * u_ref[...])
    gate = jnp.dot(z.astype(BF16), wb_ref[...], preferred_element_type=F32) + b_ref[...]
    o_ref[...] = (z * jax.nn.sigmoid(gate)).astype(o_ref.dtype)


def _s5_glu(y, proj, d_row, w_glu, b_row):
    t, w = y.shape
    tm = MM_TM
    return pl.pallas_call(
        _s5_glu_kernel,
        out_shape=jax.ShapeDtypeStruct((t, w), BF16),
        grid=(t // tm,),
        in_specs=[
            pl.BlockSpec((tm, w), lambda i: (i, 0)),
            pl.BlockSpec((tm, w), lambda i: (i, 0)),
            pl.BlockSpec((1, w), lambda i: (0, 0)),
            pl.BlockSpec((w, w), lambda i: (0, 0)),
            pl.BlockSpec((1, w), lambda i: (0, 0)),
        ],
        out_specs=pl.BlockSpec((tm, w), lambda i: (i, 0)),
        scratch_shapes=[pltpu.VMEM((w, w), BF16)],
        compiler_params=_cparams("arbitrary"),
        name="s5_glu",
    )(y, proj, d_row, w_glu, b_row)


def _lru_kernel(gate_ref, xr_ref, cw_ref, cb_ref, wa_ref, wx_ref, ba_ref, bx_ref, sp_ref, h0_ref,
                y_ref, hfin_ref, a_f, b_f, a_b, b_b):
    seq, lw = xr_ref.shape
    x = xr_ref[...]
    rows = lax.broadcasted_iota(jnp.int32, (seq, lw), 0)
    cw = cw_ref[...]
    xc = cw[2:3] * x + cb_ref[...]
    xc = xc + cw[0:1] * jnp.where(rows >= 2, pltpu.roll(x, 2, 0), 0.0)
    xc = xc + cw[1:2] * jnp.where(rows >= 1, pltpu.roll(x, 1, 0), 0.0)
    xc = xc + cw[3:4] * jnp.where(rows < seq - 1, pltpu.roll(x, seq - 1, 0), 0.0)
    for hb in range(lw // LRU_HEAD_BLOCK):
        lanes = slice(hb * LRU_HEAD_BLOCK, (hb + 1) * LRU_HEAD_BLOCK)
        xb = xc[:, lanes]
        xbb = xb.astype(BF16)
        for d, (a_s, b_s) in enumerate(((a_f, b_f), (a_b, b_b))):
            r = jax.nn.sigmoid(jnp.dot(xbb, wa_ref[d, hb], preferred_element_type=F32) + ba_ref[d:d + 1, lanes])
            gi = jax.nn.sigmoid(jnp.dot(xbb, wx_ref[d, hb], preferred_element_type=F32) + bx_ref[d:d + 1, lanes])
            a = jnp.exp(-LRU_C * r * sp_ref[d:d + 1, lanes])
            a_s[:, lanes] = a
            b_s[:, lanes] = jnp.sqrt(1.0 - a * a) * (gi * xb)

    def step(t, carry):
        hf, hb = carry
        rf = pl.ds(t, 1)
        hf = a_f[rf, :] * hf + b_f[rf, :]
        b_f[rf, :] = hf
        rb = pl.ds(seq - 1 - t, 1)
        hb = a_b[rb, :] * hb + b_b[rb, :]
        b_b[rb, :] = hb
        return hf, hb

    hf, hb = lax.fori_loop(0, seq, step, (h0_ref[0:1, :], h0_ref[1:2, :]), unroll=8)
    hfin_ref[0:1, :] = hf
    hfin_ref[1:2, :] = hb
    y_ref[...] = (_gelu(gate_ref[...]) * (b_f[...] + b_b[...])).astype(y_ref.dtype)


def _lru_mixer(proj, row0, n_seq, seq_len, lane_w, h0, conv_w, conv_b, wa_bd, wx_bd, b_a, b_x, sp):
    w = conv_w.shape[1]
    nlb = w // lane_w
    rb0 = row0 // seq_len
    hpb = lane_w // LRU_HEAD_BLOCK
    return pl.pallas_call(
        _lru_kernel,
        out_shape=(jax.ShapeDtypeStruct((n_seq * seq_len, w), BF16), jax.ShapeDtypeStruct((n_seq, 2, w), F32)),
        grid=(n_seq, nlb),
        in_specs=[
            pl.BlockSpec((seq_len, lane_w), lambda b, c: (rb0 + b, nlb + c)),
            pl.BlockSpec((seq_len, lane_w), lambda b, c: (rb0 + b, 2 * nlb + c)),
            pl.BlockSpec((conv_w.shape[0], lane_w), lambda b, c: (0, c)),
            pl.BlockSpec((1, lane_w), lambda b, c: (0, c)),
            pl.BlockSpec((2, hpb, LRU_HEAD_BLOCK, LRU_HEAD_BLOCK), lambda b, c: (0, c, 0, 0)),
            pl.BlockSpec((2, hpb, LRU_HEAD_BLOCK, LRU_HEAD_BLOCK), lambda b, c: (0, c, 0, 0)),
            pl.BlockSpec((2, lane_w), lambda b, c: (0, c)),
            pl.BlockSpec((2, lane_w), lambda b, c: (0, c)),
            pl.BlockSpec((2, lane_w), lambda b, c: (0, c)),
            pl.BlockSpec((None, 2, lane_w), lambda b, c: (b, 0, c)),
        ],
        out_specs=(pl.BlockSpec((seq_len, lane_w), lambda b, c: (b, c)),
                   pl.BlockSpec((None, 2, lane_w), lambda b, c: (b, 0, c))),
        scratch_shapes=[pltpu.VMEM((seq_len, lane_w), F32)] * 4,
        compiler_params=_cparams("arbitrary", "arbitrary"),
        name="rglru",
    )(proj, proj, conv_w, conv_b, wa_bd, wx_bd, b_a, b_x, sp, h0)


def _block_diag_heads(w):
    two, h, hd, _ = w.shape
    per = LRU_HEAD_BLOCK // hd
    wb = w.reshape(two, h // per, per, hd, hd)
    eye = jnp.eye(per, dtype=w.dtype)
    bd = jnp.einsum('dbkij,kl->dbkilj', wb, eye)
    return bd.reshape(two, h // per, LRU_HEAD_BLOCK, LRU_HEAD_BLOCK).astype(BF16)


def _softmax_pv(parts):
    m = None
    for s, _ in parts:
        mm = jnp.max(s, axis=-1, keepdims=True)
        m = mm if m is None else jnp.maximum(m, mm)
    acc, den = None, None
    for s, v in parts:
        p = jnp.exp(s - m)
        l = jnp.sum(p, axis=-1, keepdims=True)
        o = jnp.dot(p.astype(BF16), v, preferred_element_type=F32)
        acc = o if acc is None else acc + o
        den = l if den is None else den + l
    return acc / den


def _qk(q, k):
    return lax.dot_general(q, k, (((1,), (1,)), ((), ())), preferred_element_type=F32)


def _ctx_attn_kernel(q_ref, k_ref, v_ref, o_ref, *, n_heads):
    dh = q_ref.shape[1] // n_heads
    scale = dh ** -0.5
    for h in range(n_heads):
        lanes = slice(h * dh, (h + 1) * dh)
        q = q_ref[:, lanes].astype(BF16)
        k = k_ref[:, lanes].astype(BF16)
        v = v_ref[:, lanes].astype(BF16)
        o_ref[:, lanes] = _softmax_pv([(_qk(q, k) * scale, v)]).astype(o_ref.dtype)


def _ctx_attention(qkv, n_seq, seq_len, n_heads, total_rows):
    d = qkv.shape[1] // 3
    return pl.pallas_call(
        functools.partial(_ctx_attn_kernel, n_heads=n_heads),
        out_shape=jax.ShapeDtypeStruct((total_rows, d), BF16),
        grid=(n_seq,),
        in_specs=[pl.BlockSpec((seq_len, d), lambda b, cb=cb: (b, cb)) for cb in range(3)],
        out_specs=pl.BlockSpec((seq_len, d), lambda b: (b, 0)),
        compiler_params=_cparams("arbitrary"),
        name="ctx_attention",
    )(qkv, qkv, qkv)


def _nbr_attn_kernel(q_ref, k_ref, v_ref, kc_ref, vc_ref, bias_ref, o_in_ref, o_ref):
    del o_in_ref
    dh = q_ref.shape[1]
    scale = dh ** -0.5
    q = q_ref[...].astype(BF16)
    s_loc = _qk(q, k_ref[...].astype(BF16)) * scale + bias_ref[...]
    s_ctx = _qk(q, kc_ref[...].astype(BF16)) * scale
    o = _softmax_pv([(s_loc, v_ref[...].astype(BF16)), (s_ctx, vc_ref[...].astype(BF16))])
    o_ref[...] = o.astype(o_ref.dtype)


def _nbr_attention(qkv, o_all, cache_k, cache_v, layer_j, bias, row0, n_seq, seq_len, n_heads):
    d = qkv.shape[1] // 3
    dh = d // n_heads
    tq = ROW_TILE
    nq = seq_len // tq
    qb0 = row0 // tq
    kb0 = row0 // seq_len
    past = cache_k.shape[2]
    ck = cache_k.reshape(cache_k.shape[0], cache_k.shape[1], past, d)
    cv = cache_v.reshape(ck.shape)
    cache_spec = pl.BlockSpec((None, None, past, dh), lambda h, qi, b: (b, layer_j, 0, h))
    return pl.pallas_call(
        _nbr_attn_kernel,
        out_shape=jax.ShapeDtypeStruct(o_all.shape, o_all.dtype),
        grid=(n_heads, nq, n_seq),
        in_specs=[
            pl.BlockSpec((tq, dh), lambda h, qi, b: (qb0 + b * nq + qi, h)),
            pl.BlockSpec((seq_len, dh), lambda h, qi, b: (kb0 + b, n_heads + h)),
            pl.BlockSpec((seq_len, dh), lambda h, qi, b: (kb0 + b, 2 * n_heads + h)),
            cache_spec, cache_spec,
            pl.BlockSpec((None, tq, seq_len), lambda h, qi, b: (h, qi, 0)),
            pl.BlockSpec(memory_space=pl.ANY),
        ],
        out_specs=pl.BlockSpec((tq, dh), lambda h, qi, b: (qb0 + b * nq + qi, h)),
        input_output_aliases={6: 0},
        compiler_params=_cparams("arbitrary", "arbitrary", "arbitrary"),
        name="nbr_attention",
    )(qkv, qkv, qkv, ck, cv, bias, o_all)


def _nbr_bias(rpb, rows):
    kr = min(WIN_ROWS_MAX, rows)
    r_idx = np.arange(rows)
    row_start = np.clip(r_idx - kr // 2, 0, rows - kr)
    krow = np.arange(rows)[None, :]
    row_valid = (krow >= row_start[:, None]) & (krow < row_start[:, None] + kr)
    row_off = np.clip(krow - r_idx[:, None] + WIN_ROWS_MAX - 1, 0, 2 * WIN_ROWS_MAX - 2)
    c_idx = np.arange(GRID_W)
    col_start = np.clip(c_idx - WIN_COLS // 2, 0, GRID_W - WIN_COLS)
    kcol = np.arange(GRID_W)[None, :]
    col_valid = (kcol >= col_start[:, None]) & (kcol < col_start[:, None] + WIN_COLS)
    col_off = np.clip(kcol - c_idx[:, None] + WIN_COLS - 1, 0, 2 * WIN_COLS - 2)
    b = rpb.astype(F32)[:, row_off[:, None, :, None], col_off[None, :, None, :]]
    valid = row_valid[:, None, :, None] & col_valid[None, :, None, :]
    b = jnp.where(jnp.asarray(valid)[None], b, NEG_INF)
    n = rows * GRID_W
    return b.reshape(rpb.shape[0], n, n)


def _moe_gather_kernel(src_ref, nv_ref, h_hbm, o_ref, buf, sem):
    i = pl.program_id(0)
    tm = buf.shape[0]

    @pl.when(i < nv_ref[0])
    def _():
        base = i * tm

        def issue(r, c):
            t = src_ref[base + r]
            pltpu.make_async_copy(h_hbm.at[pl.ds(t, 1), :], buf.at[pl.ds(r, 1), :], sem).start()
            return c

        lax.fori_loop(0, tm, issue, 0, unroll=8)
        pltpu.make_async_copy(h_hbm.at[pl.ds(0, tm), :], buf, sem).wait()
        o_ref[...] = buf[...].astype(o_ref.dtype)


def _moe_gather(h, src, n_valid, n_rows):
    d = h.shape[1]
    tm = MOE_TM
    return pl.pallas_call(
        _moe_gather_kernel,
        out_shape=jax.ShapeDtypeStruct((n_rows, d), BF16),
        grid_spec=pltpu.PrefetchScalarGridSpec(
            num_scalar_prefetch=2,
            grid=(n_rows // tm,),
            in_specs=[pl.BlockSpec(memory_space=pl.ANY)],
            out_specs=pl.BlockSpec((tm, d), lambda i, src, nv: (i, 0)),
            scratch_shapes=[pltpu.VMEM((tm, d), F32), pltpu.SemaphoreType.DMA(())],
        ),
        compiler_params=_cparams("arbitrary"),
        name="moe_gather",
    )(src, n_valid, h)


def _new_expert(te_ref, i):
    return (i == 0) | (te_ref[i] != te_ref[jnp.maximum(i - 1, 0)])


def _moe_gu_kernel(te_ref, nv_ref, x_ref, wg_ref, wu_ref, o_ref, wgb, wub):
    i = pl.program_id(1)
    valid = i < nv_ref[0]

    @pl.when(valid & _new_expert(te_ref, i))
    def _():
        wgb[...] = wg_ref[...].astype(BF16)
        wub[...] = wu_ref[...].astype(BF16)

    @pl.when(valid)
    def _():
        x = x_ref[...]
        a = jnp.dot(x, wgb[...], preferred_element_type=F32)
        b = jnp.dot(x, wub[...], preferred_element_type=F32)
        o_ref[...] = (a * jax.nn.sigmoid(a) * b).astype(o_ref.dtype)


def _moe_gate_up(xs, w_gu_l, tile_expert, n_valid):
    p, d = xs.shape
    de = w_gu_l.shape[2] // 2
    tm, tn = MOE_TM, min(MOE_TN, de)
    nj = de // tn
    row = lambda i, nv: jnp.minimum(i, nv[0] - 1)
    return pl.pallas_call(
        _moe_gu_kernel,
        out_shape=jax.ShapeDtypeStruct((p, de), BF16),
        grid_spec=pltpu.PrefetchScalarGridSpec(
            num_scalar_prefetch=2,
            grid=(nj, p // tm),
            in_specs=[
                pl.BlockSpec((tm, d), lambda j, i, te, nv: (row(i, nv), 0)),
                pl.BlockSpec((None, d, tn), lambda j, i, te, nv: (te[i], 0, j)),
                pl.BlockSpec((None, d, tn), lambda j, i, te, nv: (te[i], 0, nj + j)),
            ],
            out_specs=pl.BlockSpec((tm, tn), lambda j, i, te, nv: (row(i, nv), j)),
            scratch_shapes=[pltpu.VMEM((d, tn), BF16), pltpu.VMEM((d, tn), BF16)],
        ),
        compiler_params=_cparams("arbitrary", "arbitrary"),
        name="moe_gate_up",
    )(tile_expert, n_valid, xs, w_gu_l, w_gu_l)


def _moe_dn_kernel(te_ref, nv_ref, h_ref, w_ref, o_ref, wb):
    i = pl.program_id(0)
    valid = i < nv_ref[0]

    @pl.when(valid & _new_expert(te_ref, i))
    def _():
        wb[...] = w_ref[...].astype(BF16)

    @pl.when(valid)
    def _():
        o_ref[...] = jnp.dot(h_ref[...], wb[...], preferred_element_type=F32)


def _moe_down(hmid, w_dn_l, tile_expert, n_valid):
    p, de = hmid.shape
    d = w_dn_l.shape[2]
    tm = MOE_TM
    row = lambda i, nv: jnp.minimum(i, nv[0] - 1)
    return pl.pallas_call(
        _moe_dn_kernel,
        out_shape=jax.ShapeDtypeStruct((p, d), F32),
        grid_spec=pltpu.PrefetchScalarGridSpec(
            num_scalar_prefetch=2,
            grid=(p // tm,),
            in_specs=[
                pl.BlockSpec((tm, de), lambda i, te, nv: (row(i, nv), 0)),
                pl.BlockSpec((None, de, d), lambda i, te, nv: (te[i], 0, 0)),
            ],
            out_specs=pl.BlockSpec((tm, d), lambda i, te, nv: (row(i, nv), 0)),
            scratch_shapes=[pltpu.VMEM((de, d), BF16)],
        ),
        compiler_params=_cparams("arbitrary"),
        name="moe_down",
    )(tile_expert, n_valid, hmid, w_dn_l)


def _moe_combine_kernel(p1_ref, p2_ref, y_hbm, x_ref, w_ref, g_ref, o_ref, buf1, buf2, sem):
    i = pl.program_id(0)
    tm = buf1.shape[0]
    base = i * tm

    def issue(r, c):
        pltpu.make_async_copy(y_hbm.at[pl.ds(p1_ref[base + r], 1), :], buf1.at[pl.ds(r, 1), :], sem).start()
        pltpu.make_async_copy(y_hbm.at[pl.ds(p2_ref[base + r], 1), :], buf2.at[pl.ds(r, 1), :], sem).start()
        return c

    lax.fori_loop(0, tm, issue, 0, unroll=8)
    pltpu.make_async_copy(y_hbm.at[pl.ds(0, tm), :], buf1, sem).wait()
    pltpu.make_async_copy(y_hbm.at[pl.ds(0, tm), :], buf2, sem).wait()
    w = w_ref[...]
    y = w[:, 0:1] * buf1[...] + w[:, 1:2] * buf2[...]
    o_ref[...] = x_ref[...] + g_ref[...] * y


def _moe_combine(y_sorted, x, wts, p1, p2, mods3, layer, lay):
    t, d = x.shape
    tm = ROW_TILE
    return pl.pallas_call(
        _moe_combine_kernel,
        out_shape=jax.ShapeDtypeStruct((t, d), F32),
        grid_spec=pltpu.PrefetchScalarGridSpec(
            num_scalar_prefetch=2,
            grid=(t // tm,),
            in_specs=[
                pl.BlockSpec(memory_space=pl.ANY),
                pl.BlockSpec((tm, d), lambda i, p1, p2: (i, 0)),
                pl.BlockSpec((tm, wts.shape[1]), lambda i, p1, p2: (i, 0)),
                pl.BlockSpec((None, 1, d), lambda i, p1, p2: (_mod_index(layer, 5, lay.mod_row(i, tm)), 0, 0)),
            ],
            out_specs=pl.BlockSpec((tm, d), lambda i, p1, p2: (i, 0)),
            scratch_shapes=[pltpu.VMEM((tm, d), F32), pltpu.VMEM((tm, d), F32), pltpu.SemaphoreType.DMA(())],
        ),
        compiler_params=_cparams("arbitrary"),
        name="moe_combine",
    )(p1, p2, y_sorted, x, wts, mods3)


def _moe_plan(route, n_experts):
    t = route.shape[1]
    tm = MOE_TM
    eid = route[0:2].astype(jnp.int32).reshape(-1)
    onehot = (eid[:, None] == jnp.arange(n_experts, dtype=jnp.int32)[None, :]).astype(jnp.int32)
    csum = jnp.cumsum(onehot, axis=0)
    rank = jnp.take_along_axis(csum, eid[:, None], axis=1)[:, 0] - 1
    counts = csum[-1]
    padded = ((counts + tm - 1) // tm) * tm
    ends = jnp.cumsum(padded)
    offs = ends - padded
    pos = offs[eid] + rank
    n_rows = 2 * t + n_experts * tm
    tok = jnp.tile(jnp.arange(t, dtype=jnp.int32), 2)
    src = jnp.zeros((n_rows,), jnp.int32).at[pos].set(tok)
    n_tiles = n_rows // tm
    n_valid = (ends[-1] // tm).astype(jnp.int32)
    starts = jnp.arange(n_tiles, dtype=jnp.int32) * tm
    te = jnp.sum((starts[:, None] >= ends[None, :]).astype(jnp.int32), axis=1)
    te_last = jnp.sum((((n_valid - 1) * tm) >= ends).astype(jnp.int32))
    te = jnp.where(starts < ends[-1], te, te_last).astype(jnp.int32)
    wts = jnp.transpose(route[2:4])
    wts = jnp.pad(wts, ((0, 0), (0, 6)))
    return src, te, n_valid.reshape(1), pos[:t], pos[t:], wts, n_rows


def _moe_layer(x, h2, route, mods3, layer, lay, w_gu_l, w_dn_l):
    n_experts = w_gu_l.shape[0]
    src, te, n_valid, p1, p2, wts, n_rows = _moe_plan(route, n_experts)
    xs = _moe_gather(h2, src, n_valid, n_rows)
    hmid = _moe_gate_up(xs, w_gu_l, te, n_valid)
    ys = _moe_down(hmid, w_dn_l, te, n_valid)
    return _moe_combine(ys, x, wts, p1, p2, mods3, layer, lay)


def kernel(x_prompt, x_sample, state_s5_re, state_s5_im, state_lru, cache_attn_k, cache_attn_v, c, c_ctx, w_ada, b_ada, norm1_g, norm2_g, final_norm_g, w_in_even, w_out_even, s5_lam_re, s5_lam_im, s5_log_dt, s5_b_re, s5_b_im, s5_c_re, s5_c_im, s5_d, s5_w_glu, s5_b_glu, lru_conv_w, lru_conv_b, lru_w_a, lru_b_a, lru_w_x, lru_b_x, lru_lam, w_qkv, w_o, rpb, w_router, b_router, w_gate_up, w_down):
    batch, seq, d = x_prompt.shape
    dec_batch, dec_seq, _ = x_sample.shape
    depth = w_ada.shape[0]
    n_heads = cache_attn_k.shape[3]
    s5_w = s5_d.shape[1]
    lru_w = lru_conv_w.shape[2]
    n_groups, n_state = s5_lam_re.shape[2], s5_lam_re.shape[3]
    assert dec_batch < MOD_ROWS
    n_ctx = batch * seq
    lay = _Layout(n_ctx, dec_batch, dec_seq)
    t = lay.total

    x = jnp.concatenate([x_prompt.reshape(n_ctx, d), x_sample.reshape(dec_batch * dec_seq, d)], axis=0)
    cvec = jnp.zeros((MOD_ROWS, d), F32).at[:dec_batch].set(c).at[dec_batch].set(c_ctx)
    mods = _ada_project(cvec, w_ada, b_ada)
    mods3 = mods.reshape(depth * MOD_ROWS * N_MOD, 1, d)
    w_router_t = jnp.transpose(w_router)
    b_router_col = b_router.reshape(-1, 1)

    s5_re_list, s5_im_list, lru_list, k_list, v_list = [], [], [], [], []
    for l in range(depth):
        j = l // 2
        h1 = _norm_mod(x, norm1_g[l].reshape(1, d), mods3, l, 0, lay, BF16)
        if l % 2 == 0:
            proj = _matmul([h1], w_in_even[j], F32)
            mats = _s5_matrices(s5_lam_re[j], s5_lam_im[j], s5_log_dt[j], s5_b_re[j], s5_b_im[j],
                                s5_c_re[j], s5_c_im[j])
            u = proj[:, :s5_w]
            zero_h0 = jnp.zeros((batch, 2, n_groups, 2 * n_state), F32)
            lat_h0 = jnp.concatenate([state_s5_re[:, j], state_s5_im[:, j]], axis=-1).astype(F32)
            y_ctx, s5_fin = _s5_scan(u[:n_ctx], zero_h0, mats, batch, seq)
            y_lat, _ = _s5_scan(u[n_ctx:], lat_h0, mats, dec_batch, dec_seq)
            s5_re_list.append(s5_fin[..., :n_state])
            s5_im_list.append(s5_fin[..., n_state:])
            y_s5 = _s5_glu(jnp.concatenate([y_ctx, y_lat], axis=0), proj, s5_d[j].reshape(1, s5_w),
                           s5_w_glu[j], s5_b_glu[j].reshape(1, s5_w))
            wa_bd = _block_diag_heads(lru_w_a[j])
            wx_bd = _block_diag_heads(lru_w_x[j])
            sp = jax.nn.softplus(-lru_lam[j].astype(F32))
            lru_args = (lru_conv_w[j], lru_conv_b[j].reshape(1, lru_w), wa_bd, wx_bd, lru_b_a[j], lru_b_x[j], sp)
            y_lru_ctx, lru_fin = _lru_mixer(proj, 0, batch, seq, lru_w, jnp.zeros((batch, 2, lru_w), F32), *lru_args)
            y_lru_lat, _ = _lru_mixer(proj, n_ctx, dec_batch, dec_seq, lru_w // 2,
                                      state_lru[:, j].astype(F32), *lru_args)
            lru_list.append(lru_fin)
            y_lru = jnp.concatenate([y_lru_ctx, y_lru_lat], axis=0)
            x = _matmul([y_s5, y_lru], w_out_even[j], F32, resid=(x, mods3, l, 2, lay))
        else:
            qkv = _matmul([h1], w_qkv[j], F32)
            k_list.append(qkv[:n_ctx, d:2 * d].reshape(batch, seq, n_heads, d // n_heads))
            v_list.append(qkv[:n_ctx, 2 * d:].reshape(batch, seq, n_heads, d // n_heads))
            o_all = _ctx_attention(qkv, batch, seq, n_heads, t)
            bias = _nbr_bias(rpb[j], dec_seq // GRID_W)
            o_all = _nbr_attention(qkv, o_all, cache_attn_k, cache_attn_v, j, bias, n_ctx, dec_batch, dec_seq, n_heads)
            x = _matmul([o_all], w_o[j], F32, resid=(x, mods3, l, 2, lay))
        h2, route = _norm_mod_route(x, norm2_g[l].reshape(1, d), mods3, l, lay, w_router_t, b_router_col)
        x = _moe_layer(x, h2, route, mods3, l, lay, w_gate_up[l], w_down[l])

    y = _final_norm(x, final_norm_g.reshape(1, d))
    y_prompt = y[:n_ctx].reshape(batch, seq, d)
    y_sample = y[n_ctx:].reshape(dec_batch, dec_seq, d)
    return (y_prompt, y_sample, jnp.stack(s5_re_list, axis=1), jnp.stack(s5_im_list, axis=1),
            jnp.stack(lru_list, axis=1), jnp.stack(k_list, axis=1), jnp.stack(v_list, axis=1))
```

```python
import functools
import math

import numpy as np
import jax
import jax.numpy as jnp
from jax import lax
from jax.experimental import pallas as pl
from jax.experimental.pallas import tpu as pltpu

F32 = jnp.float32
BF16 = jnp.bfloat16

NORM_EPS = 1e-6
NEG_INF = -1e30
S5_GROUP = 16
S5_CHUNK = 16
LRU_C = 8.0
LRU_HEAD_BLOCK = 256
N_EXPERT_GROUPS = 4
WIN_ROWS_MAX = 8
WIN_COLS = 16
GRID_W = 64
N_MOD = 6
MOD_ROWS = 8
VMEM_LIMIT = 52 * 1024 * 1024
ROW_TILE = 256
MM_TM = 1024
MM_TN = 1024
MM_TN_RESID = 512
MOE_TM = 512
MOE_TN = 512


def _cparams(*sem):
    return pltpu.CompilerParams(dimension_semantics=sem, vmem_limit_bytes=VMEM_LIMIT)


def _gelu(x):
    return 0.5 * x * (1.0 + jnp.tanh(math.sqrt(2.0 / math.pi) * (x + 0.044715 * (x * x * x))))


class _Layout:
    def __init__(self, n_ctx_rows, dec_batch, dec_seq):
        self.n_ctx = n_ctx_rows
        self.dec_batch = dec_batch
        self.dec_seq = dec_seq
        self.total = n_ctx_rows + dec_batch * dec_seq

    def mod_row(self, i, tm):
        nct = self.n_ctx // tm
        per = self.dec_seq // tm
        return jnp.where(i < nct, self.dec_batch, (i - nct) // per)


def _mod_index(layer, which, row):
    return (layer * MOD_ROWS + row) * N_MOD + which


def _ada_kernel(c_ref, w_ref, b_ref, o_ref):
    c = c_ref[...]
    s = (c * jax.nn.sigmoid(c)).astype(BF16)
    o_ref[...] = jnp.dot(s, w_ref[...].astype(BF16), preferred_element_type=F32) + b_ref[...]


def _ada_project(cvec, w_ada, b_ada):
    depth, d, n = w_ada.shape
    tn = MM_TN
    return pl.pallas_call(
        _ada_kernel,
        out_shape=jax.ShapeDtypeStruct((depth, MOD_ROWS, n), F32),
        grid=(depth, n // tn),
        in_specs=[
            pl.BlockSpec((MOD_ROWS, d), lambda l, j: (0, 0)),
            pl.BlockSpec((None, d, tn), lambda l, j: (l, 0, j)),
            pl.BlockSpec((None, 1, tn), lambda l, j: (l, 0, j)),
        ],
        out_specs=pl.BlockSpec((None, MOD_ROWS, tn), lambda l, j: (l, 0, j)),
        compiler_params=_cparams("arbitrary", "arbitrary"),
        name="ada_project",
    )(cvec, w_ada, b_ada.reshape(depth, 1, n))


def _rms(x, g):
    ms = jnp.mean(x * x, axis=-1, keepdims=True)
    return x * lax.rsqrt(ms + NORM_EPS) * g


def _norm_mod_kernel(x_ref, g_ref, sh_ref, sc_ref, o_ref):
    y = _rms(x_ref[...], g_ref[...])
    o_ref[...] = (y * (1.0 + sc_ref[...]) + sh_ref[...]).astype(o_ref.dtype)


def _norm_mod(x, g_row, mods3, layer, which_shift, lay, out_dtype):
    t, d = x.shape
    tm = ROW_TILE
    row = lambda i: lay.mod_row(i, tm)
    return pl.pallas_call(
        _norm_mod_kernel,
        out_shape=jax.ShapeDtypeStruct((t, d), out_dtype),
        grid=(t // tm,),
        in_specs=[
            pl.BlockSpec((tm, d), lambda i: (i, 0)),
            pl.BlockSpec((1, d), lambda i: (0, 0)),
            pl.BlockSpec((None, 1, d), lambda i: (_mod_index(layer, which_shift, row(i)), 0, 0)),
            pl.BlockSpec((None, 1, d), lambda i: (_mod_index(layer, which_shift + 1, row(i)), 0, 0)),
        ],
        out_specs=pl.BlockSpec((tm, d), lambda i: (i, 0)),
        compiler_params=_cparams("arbitrary"),
        name="norm_mod",
    )(x, g_row, mods3, mods3)


def _top2_of4(a):
    m1 = jnp.maximum(jnp.maximum(a[0], a[1]), jnp.maximum(a[2], a[3]))
    i1 = jnp.where(a[0] == m1, 0, jnp.where(a[1] == m1, 1, jnp.where(a[2] == m1, 2, 3)))
    b = [jnp.where(i1 == k, -jnp.inf, a[k]) for k in range(4)]
    m2 = jnp.maximum(jnp.maximum(b[0], b[1]), jnp.maximum(b[2], b[3]))
    i2 = jnp.where(b[0] == m2, 0, jnp.where(b[1] == m2, 1, jnp.where(b[2] == m2, 2, 3)))
    return m1 + m2, i1, i2


def _norm_mod_route_kernel(x_ref, g_ref, sh_ref, sc_ref, wr_ref, br_ref, o_ref, r_ref):
    y = _rms(x_ref[...], g_ref[...])
    h = y * (1.0 + sc_ref[...]) + sh_ref[...]
    o_ref[...] = h.astype(o_ref.dtype)
    logits = lax.dot_general(wr_ref[...], h, (((1,), (1,)), ((), ())),
                             precision=lax.Precision.HIGHEST, preferred_element_type=F32)
    scores = jax.nn.sigmoid(logits)
    sel = scores + br_ref[...]
    n_e = scores.shape[0]
    per = n_e // N_EXPERT_GROUPS
    sel_rows = [sel[e:e + 1, :] for e in range(n_e)]
    score_rows = [scores[e:e + 1, :] for e in range(n_e)]
    gs, i1s, i2s = [], [], []
    for gi in range(N_EXPERT_GROUPS):
        s, i1, i2 = _top2_of4(sel_rows[gi * per:(gi + 1) * per])
        gs.append(s)
        i1s.append(i1)
        i2s.append(i2)
    gmax = jnp.maximum(jnp.maximum(gs[0], gs[1]), jnp.maximum(gs[2], gs[3]))
    gsel = jnp.where(gs[0] == gmax, 0, jnp.where(gs[1] == gmax, 1, jnp.where(gs[2] == gmax, 2, 3)))
    l1 = jnp.where(gsel == 0, i1s[0], jnp.where(gsel == 1, i1s[1], jnp.where(gsel == 2, i1s[2], i1s[3])))
    l2 = jnp.where(gsel == 0, i2s[0], jnp.where(gsel == 1, i2s[1], jnp.where(gsel == 2, i2s[2], i2s[3])))
    e1 = gsel * per + l1
    e2 = gsel * per + l2
    w1 = jnp.zeros_like(gmax)
    w2 = jnp.zeros_like(gmax)
    for e in range(n_e):
        w1 = jnp.where(e1 == e, score_rows[e], w1)
        w2 = jnp.where(e2 == e, score_rows[e], w2)
    wsum = w1 + w2
    zero = jnp.zeros_like(gmax)
    r_ref[...] = jnp.concatenate(
        [e1.astype(F32), e2.astype(F32), w1 / wsum, w2 / wsum, zero, zero, zero, zero], axis=0)


def _norm_mod_route(x, g_row, mods3, layer, lay, w_router_t, b_router_col):
    t, d = x.shape
    tm = ROW_TILE
    n_e = w_router_t.shape[0]
    row = lambda i: lay.mod_row(i, tm)
    return pl.pallas_call(
        _norm_mod_route_kernel,
        out_shape=(jax.ShapeDtypeStruct((t, d), F32), jax.ShapeDtypeStruct((8, t), F32)),
        grid=(t // tm,),
        in_specs=[
            pl.BlockSpec((tm, d), lambda i: (i, 0)),
            pl.BlockSpec((1, d), lambda i: (0, 0)),
            pl.BlockSpec((None, 1, d), lambda i: (_mod_index(layer, 3, row(i)), 0, 0)),
            pl.BlockSpec((None, 1, d), lambda i: (_mod_index(layer, 4, row(i)), 0, 0)),
            pl.BlockSpec((n_e, d), lambda i: (0, 0)),
            pl.BlockSpec((n_e, 1), lambda i: (0, 0)),
        ],
        out_specs=(pl.BlockSpec((tm, d), lambda i: (i, 0)), pl.BlockSpec((8, tm), lambda i: (0, i))),
        compiler_params=_cparams("arbitrary"),
        name="norm_mod_route",
    )(x, g_row, mods3, mods3, w_router_t, b_router_col)


def _final_norm_kernel(x_ref, g_ref, o_ref):
    o_ref[...] = _rms(x_ref[...], g_ref[...])


def _final_norm(x, g_row, row0, n_rows):
    d = x.shape[1]
    tm = ROW_TILE
    return pl.pallas_call(
        _final_norm_kernel,
        out_shape=jax.ShapeDtypeStruct((n_rows, d), F32),
        grid=(n_rows // tm,),
        in_specs=[pl.BlockSpec((tm, d), lambda i: (row0 // tm + i, 0)), pl.BlockSpec((1, d), lambda i: (0, 0))],
        out_specs=pl.BlockSpec((tm, d), lambda i: (i, 0)),
        compiler_params=_cparams("arbitrary"),
        name="final_norm",
    )(x, g_row)


def _mm_kernel(*refs, n_a, resid):
    a_refs = refs[:n_a]
    w_ref = refs[n_a]
    pos = n_a + 1
    if resid:
        x_ref, g_ref = refs[pos], refs[pos + 1]
        pos += 2
    o_ref, wb_ref = refs[pos], refs[pos + 1]

    @pl.when(pl.program_id(1) == 0)
    def _():
        wb_ref[...] = w_ref[...].astype(BF16)

    acc = None
    k0 = 0
    for a_ref in a_refs:
        ka = a_ref.shape[1]
        part = jnp.dot(a_ref[...], wb_ref[k0:k0 + ka, :], preferred_element_type=F32)
        acc = part if acc is None else acc + part
        k0 += ka
    if resid:
        o_ref[...] = x_ref[...] + g_ref[...] * acc
    else:
        o_ref[...] = acc.astype(o_ref.dtype)


def _matmul(a_list, w, out_dtype, resid=None):
    m = a_list[0].shape[0]
    k, n = w.shape
    tm = MM_TM
    tn = min(MM_TN_RESID if resid is not None else MM_TN, n)
    in_specs = [pl.BlockSpec((tm, a.shape[1]), lambda j, i: (i, 0)) for a in a_list]
    in_specs.append(pl.BlockSpec((k, tn), lambda j, i: (0, j)))
    args = list(a_list) + [w]
    if resid is not None:
        x, mods3, layer, which, lay = resid
        in_specs.append(pl.BlockSpec((tm, tn), lambda j, i: (i, j)))
        in_specs.append(pl.BlockSpec(
            (None, 1, tn), lambda j, i: (_mod_index(layer, which, lay.mod_row(i, tm)), 0, j)))
        args += [x, mods3]
    return pl.pallas_call(
        functools.partial(_mm_kernel, n_a=len(a_list), resid=resid is not None),
        out_shape=jax.ShapeDtypeStruct((m, n), out_dtype),
        grid=(n // tn, m // tm),
        in_specs=in_specs,
        out_specs=pl.BlockSpec((tm, tn), lambda j, i: (i, j)),
        scratch_shapes=[pltpu.VMEM((k, tn), BF16)],
        compiler_params=_cparams("arbitrary", "arbitrary"),
        name="matmul_resid" if resid is not None else "matmul",
    )(*args)


def _cmul(ar, ai, br, bi):
    return ar * br - ai * bi, ar * bi + ai * br


def _s5_matrices(lam_re, lam_im, log_dt, b_re, b_im, c_re, c_im):
    hi = lax.Precision.HIGHEST
    s = S5_CHUNK
    f = lambda z: z.astype(F32)
    lam_re, lam_im, log_dt, b_re, b_im, c_re, c_im = map(f, (lam_re, lam_im, log_dt, b_re, b_im, c_re, c_im))
    dt = jnp.exp(log_dt)[..., None]
    kk = jnp.arange(s + 1, dtype=F32)[:, None, None, None]
    mag = jnp.exp(kk * (lam_re * dt)[None])
    ph = kk * (lam_im * dt)[None]
    pw_re, pw_im = mag * jnp.cos(ph), mag * jnp.sin(ph)
    a_re, a_im = pw_re[1], pw_im[1]
    den = lam_re * lam_re + lam_im * lam_im
    q_re, q_im = _cmul(a_re - 1.0, a_im, lam_re / den, -lam_im / den)
    bb_re, bb_im = _cmul(q_re[..., None], q_im[..., None], b_re, b_im)
    cp_re, cp_im = _cmul(c_re[:, :, None], c_im[:, :, None],
                         jnp.moveaxis(pw_re, 0, 2)[:, :, :, None, :], jnp.moveaxis(pw_im, 0, 2)[:, :, :, None, :])
    kern = (jnp.einsum('dgtip,dgpj->dgtij', cp_re, bb_re, precision=hi)
            - jnp.einsum('dgtip,dgpj->dgtij', cp_im, bb_im, precision=hi))
    sp = np.arange(s)[:, None]
    so = np.arange(s)[None, :]
    lag_f = np.clip(so - sp, 0, s)
    lag_b = np.clip(sp - so, 0, s)
    kf = jnp.where(jnp.asarray(so >= sp)[None, :, :, None, None], kern[0][:, lag_f], 0.0)
    kb = jnp.where(jnp.asarray(sp >= so)[None, :, :, None, None], kern[1][:, lag_b], 0.0)
    g = kern.shape[1]
    n = s * S5_GROUP
    tmat = jnp.transpose(kf + kb, (0, 1, 4, 2, 3)).reshape(g, n, n)

    def e_mat(d, powers):
        pr = jnp.moveaxis(pw_re[powers, d], 0, 1)[:, :, :, None]
        pi = jnp.moveaxis(pw_im[powers, d], 0, 1)[:, :, :, None]
        er, ei = _cmul(pr, pi, bb_re[d][:, None], bb_im[d][:, None])
        er = jnp.transpose(er, (0, 1, 3, 2)).reshape(g, n, -1)
        ei = jnp.transpose(ei, (0, 1, 3, 2)).reshape(g, n, -1)
        return jnp.concatenate([er, ei], axis=-1)

    def c_mat(d, powers):
        cr = jnp.transpose(cp_re[d][:, powers], (0, 3, 1, 2)).reshape(g, -1, n)
        ci = jnp.transpose(cp_im[d][:, powers], (0, 3, 1, 2)).reshape(g, -1, n)
        return jnp.concatenate([cr, -ci], axis=1)

    swap = lambda e: jnp.concatenate([e[..., e.shape[-1] // 2:], e[..., :e.shape[-1] // 2]], axis=-1)
    e_f = e_mat(0, np.arange(s - 1, -1, -1))
    e_b = e_mat(1, np.arange(s))
    e_all = jnp.concatenate([e_f, swap(e_f), e_b, swap(e_b)], axis=-1)
    c_all = jnp.concatenate([c_mat(0, np.arange(1, s + 1)), c_mat(1, np.arange(s, 0, -1))], axis=1)
    dec_r = jnp.concatenate([pw_re[s], pw_re[s]], axis=-1)
    dec_i = jnp.concatenate([-pw_im[s], pw_im[s]], axis=-1)
    decay = jnp.stack([dec_r, dec_i], axis=2)[:, :, :, None, :]
    return tmat.astype(BF16), e_all.astype(BF16), c_all.astype(BF16), decay


def _s5_kernel(u_ref, t_ref, e_ref, c_ref, dec_ref, h0_ref, y_ref, hfin_ref, z_ref, hs_ref, *, nb, nc):
    u = u_ref[...]
    z_ref[...] = jnp.dot(u, e_ref[...], preferred_element_type=F32)
    p2 = hs_ref.shape[1] // 2
    ar_f, ai_f, ar_b, ai_b = dec_ref[0, 0], dec_ref[0, 1], dec_ref[1, 0], dec_ref[1, 1]
    hf, hfs, hb, hbs = h0_ref[0], h0_ref[1], h0_ref[2], h0_ref[3]
    for c in range(nc):
        rf = pl.ds(c * nb, nb)
        rb = pl.ds((nc - 1 - c) * nb, nb)
        hs_ref[rf, 0:p2] = hf
        hs_ref[rb, p2:2 * p2] = hb
        hf, hfs = (ar_f * hf + ai_f * hfs + z_ref[rf, 0:p2],
                   ar_f * hfs - ai_f * hf + z_ref[rf, p2:2 * p2])
        hb, hbs = (ar_b * hb + ai_b * hbs + z_ref[rb, 2 * p2:3 * p2],
                   ar_b * hbs - ai_b * hb + z_ref[rb, 3 * p2:4 * p2])
    hfin_ref[0] = hf
    hfin_ref[1] = hb
    y = jnp.dot(u, t_ref[...], preferred_element_type=F32)
    y_ref[...] = y + jnp.dot(hs_ref[...].astype(BF16), c_ref[...], preferred_element_type=F32)


def _s5_scan(u_tok, h0, mats, n_seq, seq_len):
    tmat, e_all, c_all, decay = mats
    g = tmat.shape[0]
    n = tmat.shape[1]
    p2 = decay.shape[-1]
    s = S5_CHUNK
    nc = seq_len // s
    nb = -(-n_seq // 8) * 8
    r = nc * nb
    u5 = u_tok.astype(BF16).reshape(n_seq, nc, s, g, S5_GROUP)
    up = jnp.transpose(u5, (3, 1, 0, 2, 4))
    up = jnp.pad(up, ((0, 0), (0, 0), (0, nb - n_seq), (0, 0), (0, 0))).reshape(g, r, n)
    h0s = jnp.concatenate([h0[..., p2 // 2:], h0[..., :p2 // 2]], axis=-1)
    h04 = jnp.stack([h0[:, 0], h0s[:, 0], h0[:, 1], h0s[:, 1]], axis=1)
    h0p = jnp.pad(jnp.transpose(h04, (2, 1, 0, 3)), ((0, 0), (0, 0), (0, nb - n_seq), (0, 0)))
    wspec = lambda shape: pl.BlockSpec((None,) + shape, lambda gi: (gi, 0, 0))
    y, hfin = pl.pallas_call(
        functools.partial(_s5_kernel, nb=nb, nc=nc),
        out_shape=(jax.ShapeDtypeStruct((g, r, n), F32), jax.ShapeDtypeStruct((g, 2, nb, p2), F32)),
        grid=(g,),
        in_specs=[
            wspec((r, n)), wspec((n, n)), wspec((n, 4 * p2)), wspec((2 * p2, n)),
            pl.BlockSpec((2, None, 2, 1, p2), lambda gi: (0, gi, 0, 0, 0)),
            pl.BlockSpec((None, 4, nb, p2), lambda gi: (gi, 0, 0, 0)),
        ],
        out_specs=(wspec((r, n)), pl.BlockSpec((None, 2, nb, p2), lambda gi: (gi, 0, 0, 0))),
        scratch_shapes=[pltpu.VMEM((r, 4 * p2), F32), pltpu.VMEM((r, 2 * p2), F32)],
        compiler_params=_cparams("arbitrary"),
        name="s5_scan",
    )(up, tmat, e_all, c_all, decay, h0p)
    y5 = y.reshape(g, nc, nb, s, S5_GROUP)[:, :, :n_seq]
    y_tok = jnp.transpose(y5, (2, 1, 3, 0, 4)).reshape(n_seq * seq_len, g * S5_GROUP)
    return y_tok, jnp.transpose(hfin[:, :, :n_seq], (2, 1, 0, 3))


def _s5_glu_kernel(y_ref, u_ref, d_ref, w_ref, b_ref, o_ref, wb_ref):
    @pl.when(pl.program_id(0) == 0)
    def _():
        wb_ref[...] = w_ref[...].astype(BF16)

    z = _gelu(y_ref[...] + d_ref[...] * u_ref[...])
    gate = jnp.dot(z.astype(BF16), wb_ref[...], preferred_element_type=F32) + b_ref[...]
    o_ref[...] = (z * jax.nn.sigmoid(gate)).astype(o_ref.dtype)


def _s5_glu(y, proj, d_row, w_glu, b_row):
    t, w = y.shape
    tm = MM_TM
    return pl.pallas_call(
        _s5_glu_kernel,
        out_shape=jax.ShapeDtypeStruct((t, w), BF16),
        grid=(t // tm,),
        in_specs=[
            pl.BlockSpec((tm, w), lambda i: (i, 0)),
            pl.BlockSpec((tm, w), lambda i: (i, 0)),
            pl.BlockSpec((1, w), lambda i: (0, 0)),
            pl.BlockSpec((w, w), lambda i: (0, 0)),
            pl.BlockSpec((1, w), lambda i: (0, 0)),
        ],
        out_specs=pl.BlockSpec((tm, w), lambda i: (i, 0)),
        scratch_shapes=[pltpu.VMEM((w, w), BF16)],
        compiler_params=_cparams("arbitrary"),
        name="s5_glu",
    )(y, proj, d_row, w_glu, b_row)


def _lru_kernel(gate_ref, xr_ref, cw_ref, cb_ref, wa_ref, wx_ref, ba_ref, bx_ref, sp_ref, h0_ref,
                y_ref, hfin_ref, a_f, b_f, a_b, b_b):
    seq, lw = xr_ref.shape
    x = xr_ref[...]
    rows = lax.broadcasted_iota(jnp.int32, (seq, lw), 0)
    cw = cw_ref[...]
    xc = cw[2:3] * x + cb_ref[...]
    xc = xc + cw[0:1] * jnp.where(rows >= 2, pltpu.roll(x, 2, 0), 0.0)
    xc = xc + cw[1:2] * jnp.where(rows >= 1, pltpu.roll(x, 1, 0), 0.0)
    xc = xc + cw[3:4] * jnp.where(rows < seq - 1, pltpu.roll(x, seq - 1, 0), 0.0)
    for hb in range(lw // LRU_HEAD_BLOCK):
        lanes = slice(hb * LRU_HEAD_BLOCK, (hb + 1) * LRU_HEAD_BLOCK)
        xb = xc[:, lanes]
        xbb = xb.astype(BF16)
        for d, (a_s, b_s) in enumerate(((a_f, b_f), (a_b, b_b))):
            r = jax.nn.sigmoid(jnp.dot(xbb, wa_ref[d, hb], preferred_element_type=F32) + ba_ref[d:d + 1, lanes])
            gi = jax.nn.sigmoid(jnp.dot(xbb, wx_ref[d, hb], preferred_element_type=F32) + bx_ref[d:d + 1, lanes])
            a = jnp.exp(-LRU_C * r * sp_ref[d:d + 1, lanes])
            a_s[:, lanes] = a
            b_s[:, lanes] = jnp.sqrt(1.0 - a * a) * (gi * xb)

    def step(t, carry):
        hf, hb = carry
        rf = pl.ds(t, 1)
        hf = a_f[rf, :] * hf + b_f[rf, :]
        b_f[rf, :] = hf
        rb = pl.ds(seq - 1 - t, 1)
        hb = a_b[rb, :] * hb + b_b[rb, :]
        b_b[rb, :] = hb
        return hf, hb

    hf, hb = lax.fori_loop(0, seq, step, (h0_ref[0:1, :], h0_ref[1:2, :]), unroll=8)
    hfin_ref[0:1, :] = hf
    hfin_ref[1:2, :] = hb
    y_ref[...] = (_gelu(gate_ref[...]) * (b_f[...] + b_b[...])).astype(y_ref.dtype)


def _lru_mixer(proj, row0, n_seq, seq_len, lane_w, h0, conv_w, conv_b, wa_bd, wx_bd, b_a, b_x, sp):
    w = conv_w.shape[1]
    nlb = w // lane_w
    rb0 = row0 // seq_len
    hpb = lane_w // LRU_HEAD_BLOCK
    return pl.pallas_call(
        _lru_kernel,
        out_shape=(jax.ShapeDtypeStruct((n_seq * seq_len, w), BF16), jax.ShapeDtypeStruct((n_seq, 2, w), F32)),
        grid=(n_seq, nlb),
        in_specs=[
            pl.BlockSpec((seq_len, lane_w), lambda b, c: (rb0 + b, nlb + c)),
            pl.BlockSpec((seq_len, lane_w), lambda b, c: (rb0 + b, 2 * nlb + c)),
            pl.BlockSpec((conv_w.shape[0], lane_w), lambda b, c: (0, c)),
            pl.BlockSpec((1, lane_w), lambda b, c: (0, c)),
            pl.BlockSpec((2, hpb, LRU_HEAD_BLOCK, LRU_HEAD_BLOCK), lambda b, c: (0, c, 0, 0)),
            pl.BlockSpec((2, hpb, LRU_HEAD_BLOCK, LRU_HEAD_BLOCK), lambda b, c: (0, c, 0, 0)),
            pl.BlockSpec((2, lane_w), lambda b, c: (0, c)),
            pl.BlockSpec((2, lane_w), lambda b, c: (0, c)),
            pl.BlockSpec((2, lane_w), lambda b, c: (0, c)),
            pl.BlockSpec((None, 2, lane_w), lambda b, c: (b, 0, c)),
        ],
        out_specs=(pl.BlockSpec((seq_len, lane_w), lambda b, c: (b, c)),
                   pl.BlockSpec((None, 2, lane_w), lambda b, c: (b, 0, c))),
        scratch_shapes=[pltpu.VMEM((seq_len, lane_w), F32)] * 4,
        compiler_params=_cparams("arbitrary", "arbitrary"),
        name="rglru",
    )(proj, proj, conv_w, conv_b, wa_bd, wx_bd, b_a, b_x, sp, h0)


def _block_diag_heads(w):
    two, h, hd, _ = w.shape
    per = LRU_HEAD_BLOCK // hd
    wb = w.reshape(two, h // per, per, hd, hd)
    eye = jnp.eye(per, dtype=w.dtype)
    bd = jnp.einsum('dbkij,kl->dbkilj', wb, eye)
    return bd.reshape(two, h // per, LRU_HEAD_BLOCK, LRU_HEAD_BLOCK).astype(BF16)


def _softmax_pv(parts):
    m = None
    for s, _ in parts:
        mm = jnp.max(s, axis=-1, keepdims=True)
        m = mm if m is None else jnp.maximum(m, mm)
    acc, den = None, None
    for s, v in parts:
        p = jnp.exp(s - m)
        l = jnp.sum(p, axis=-1, keepdims=True)
        o = jnp.dot(p.astype(BF16), v, preferred_element_type=F32)
        acc = o if acc is None else acc + o
        den = l if den is None else den + l
    return acc / den


def _qk(q, k):
    return lax.dot_general(q, k, (((1,), (1,)), ((), ())), preferred_element_type=F32)


def _ctx_attn_kernel(q_ref, k_ref, v_ref, *rest, n_heads):
    o_ref, ck_ref, cv_ref = rest[-3:]
    dh = q_ref.shape[1] // n_heads
    scale = dh ** -0.5
    ck_ref[...] = k_ref[...]
    cv_ref[...] = v_ref[...]
    for h in range(n_heads):
        lanes = slice(h * dh, (h + 1) * dh)
        q = q_ref[:, lanes].astype(BF16)
        k = k_ref[:, lanes].astype(BF16)
        v = v_ref[:, lanes].astype(BF16)
        o_ref[:, lanes] = _softmax_pv([(_qk(q, k) * scale, v)]).astype(o_ref.dtype)


def _ctx_attention(qkv, n_seq, seq_len, n_heads, total_rows, layer_j, n_attn_layers, caches):
    d = qkv.shape[1] // 3
    cache_shape = jax.ShapeDtypeStruct((n_seq, n_attn_layers, seq_len, d), F32)
    cache_spec = pl.BlockSpec((None, None, seq_len, d), lambda b: (b, layer_j, 0, 0))
    in_specs = [pl.BlockSpec((seq_len, d), lambda b, cb=cb: (b, cb)) for cb in range(3)]
    args = [qkv, qkv, qkv]
    aliases = {}
    if caches is not None:
        in_specs += [pl.BlockSpec(memory_space=pl.ANY)] * 2
        args += list(caches)
        aliases = {3: 1, 4: 2}
    return pl.pallas_call(
        functools.partial(_ctx_attn_kernel, n_heads=n_heads),
        out_shape=(jax.ShapeDtypeStruct((total_rows, d), BF16), cache_shape, cache_shape),
        grid=(n_seq,),
        in_specs=in_specs,
        out_specs=(pl.BlockSpec((seq_len, d), lambda b: (b, 0)), cache_spec, cache_spec),
        input_output_aliases=aliases,
        compiler_params=_cparams("arbitrary"),
        name="ctx_attention",
    )(*args)


def _nbr_attn_kernel(q_ref, k_ref, v_ref, kc_ref, vc_ref, bias_ref, o_in_ref, o_ref):
    del o_in_ref
    dh = q_ref.shape[1]
    scale = dh ** -0.5
    q = q_ref[...].astype(BF16)
    s_loc = _qk(q, k_ref[...].astype(BF16)) * scale + bias_ref[...]
    s_ctx = _qk(q, kc_ref[...].astype(BF16)) * scale
    o = _softmax_pv([(s_loc, v_ref[...].astype(BF16)), (s_ctx, vc_ref[...].astype(BF16))])
    o_ref[...] = o.astype(o_ref.dtype)


def _nbr_attention(qkv, o_all, cache_k, cache_v, layer_j, bias, row0, n_seq, seq_len, n_heads):
    d = qkv.shape[1] // 3
    dh = d // n_heads
    tq = ROW_TILE
    nq = seq_len // tq
    qb0 = row0 // tq
    kb0 = row0 // seq_len
    past = cache_k.shape[2]
    ck = cache_k.reshape(cache_k.shape[0], cache_k.shape[1], past, d)
    cv = cache_v.reshape(ck.shape)
    cache_spec = pl.BlockSpec((None, None, past, dh), lambda h, qi, b: (b, layer_j, 0, h))
    return pl.pallas_call(
        _nbr_attn_kernel,
        out_shape=jax.ShapeDtypeStruct(o_all.shape, o_all.dtype),
        grid=(n_heads, nq, n_seq),
        in_specs=[
            pl.BlockSpec((tq, dh), lambda h, qi, b: (qb0 + b * nq + qi, h)),
            pl.BlockSpec((seq_len, dh), lambda h, qi, b: (kb0 + b, n_heads + h)),
            pl.BlockSpec((seq_len, dh), lambda h, qi, b: (kb0 + b, 2 * n_heads + h)),
            cache_spec, cache_spec,
            pl.BlockSpec((None, tq, seq_len), lambda h, qi, b: (h, qi, 0)),
            pl.BlockSpec(memory_space=pl.ANY),
        ],
        out_specs=pl.BlockSpec((tq, dh), lambda h, qi, b: (qb0 + b * nq + qi, h)),
        input_output_aliases={6: 0},
        compiler_params=_cparams("arbitrary", "arbitrary", "arbitrary"),
        name="nbr_attention",
    )(qkv, qkv, qkv, ck, cv, bias, o_all)


def _nbr_bias(rpb, rows):
    kr = min(WIN_ROWS_MAX, rows)
    r_idx = np.arange(rows)
    row_start = np.clip(r_idx - kr // 2, 0, rows - kr)
    krow = np.arange(rows)[None, :]
    row_valid = (krow >= row_start[:, None]) & (krow < row_start[:, None] + kr)
    row_off = np.clip(krow - r_idx[:, None] + WIN_ROWS_MAX - 1, 0, 2 * WIN_ROWS_MAX - 2)
    c_idx = np.arange(GRID_W)
    col_start = np.clip(c_idx - WIN_COLS // 2, 0, GRID_W - WIN_COLS)
    kcol = np.arange(GRID_W)[None, :]
    col_valid = (kcol >= col_start[:, None]) & (kcol < col_start[:, None] + WIN_COLS)
    col_off = np.clip(kcol - c_idx[:, None] + WIN_COLS - 1, 0, 2 * WIN_COLS - 2)
    hi = lax.Precision.HIGHEST
    row_sel = jnp.asarray(np.eye(2 * WIN_ROWS_MAX - 1, dtype=np.float32)[row_off])
    col_sel = jnp.asarray(np.eye(2 * WIN_COLS - 1, dtype=np.float32)[col_off])
    part = jnp.einsum('rki,hij->hrkj', row_sel, rpb.astype(F32), precision=hi)
    b = jnp.einsum('hrkj,cmj->hrckm', part, col_sel, precision=hi)
    valid = row_valid[:, None, :, None] & col_valid[None, :, None, :]
    b = jnp.where(jnp.asarray(valid)[None], b, NEG_INF)
    n = rows * GRID_W
    return b.reshape(rpb.shape[0], n, n)


def _moe_gather_kernel(src_ref, nv_ref, h_hbm, o_ref, buf, sem):
    i = pl.program_id(0)
    tm = buf.shape[1]
    slot = i % 2

    def issue_tile(tile, s):
        base = tile * tm

        def issue(r, c):
            t = src_ref[base + r]
            pltpu.make_async_copy(h_hbm.at[pl.ds(t, 1), :], buf.at[s, pl.ds(r, 1), :], sem.at[s]).start()
            return c

        lax.fori_loop(0, tm, issue, 0, unroll=8)

    @pl.when(i == 0)
    def _():
        issue_tile(0, 0)

    @pl.when(i + 1 < nv_ref[0])
    def _():
        issue_tile(i + 1, 1 - slot)

    @pl.when(i < nv_ref[0])
    def _():
        pltpu.make_async_copy(h_hbm.at[pl.ds(0, tm), :], buf.at[slot], sem.at[slot]).wait()
        o_ref[...] = buf[slot].astype(o_ref.dtype)

    @pl.when(i >= nv_ref[0])
    def _():
        o_ref[...] = jnp.zeros_like(o_ref)


def _moe_gather(h, src, n_valid, n_rows):
    d = h.shape[1]
    tm = MOE_TM
    return pl.pallas_call(
        _moe_gather_kernel,
        out_shape=jax.ShapeDtypeStruct((n_rows, d), BF16),
        grid_spec=pltpu.PrefetchScalarGridSpec(
            num_scalar_prefetch=2,
            grid=(n_rows // tm,),
            in_specs=[pl.BlockSpec(memory_space=pl.ANY)],
            out_specs=pl.BlockSpec((tm, d), lambda i, src, nv: (i, 0)),
            scratch_shapes=[pltpu.VMEM((2, tm, d), F32), pltpu.SemaphoreType.DMA((2,))],
        ),
        compiler_params=_cparams("arbitrary"),
        name="moe_gather",
    )(src, n_valid, h)


def _new_expert(te_ref, i):
    return (i == 0) | (te_ref[i] != te_ref[jnp.maximum(i - 1, 0)])


def _moe_gu_kernel(te_ref, nv_ref, x_ref, wg_ref, wu_ref, o_ref, wgb, wub):
    i = pl.program_id(1)
    valid = i < nv_ref[0]

    @pl.when(valid & _new_expert(te_ref, i))
    def _():
        wgb[...] = wg_ref[...].astype(BF16)
        wub[...] = wu_ref[...].astype(BF16)

    @pl.when(valid)
    def _():
        x = x_ref[...]
        a = jnp.dot(x, wgb[...], preferred_element_type=F32)
        b = jnp.dot(x, wub[...], preferred_element_type=F32)
        o_ref[...] = (a * jax.nn.sigmoid(a) * b).astype(o_ref.dtype)

    @pl.when(jnp.logical_not(valid))
    def _():
        o_ref[...] = jnp.zeros_like(o_ref)


def _moe_gate_up(xs, w_gu_l, tile_expert, n_valid):
    p, d = xs.shape
    de = w_gu_l.shape[2] // 2
    tm, tn = MOE_TM, min(MOE_TN, de)
    nj = de // tn
    row = lambda i, nv: jnp.minimum(i, nv[0] - 1)
    return pl.pallas_call(
        _moe_gu_kernel,
        out_shape=jax.ShapeDtypeStruct((p, de), BF16),
        grid_spec=pltpu.PrefetchScalarGridSpec(
            num_scalar_prefetch=2,
            grid=(nj, p // tm),
            in_specs=[
                pl.BlockSpec((tm, d), lambda j, i, te, nv: (row(i, nv), 0)),
                pl.BlockSpec((None, d, tn), lambda j, i, te, nv: (te[i], 0, j)),
                pl.BlockSpec((None, d, tn), lambda j, i, te, nv: (te[i], 0, nj + j)),
            ],
            out_specs=pl.BlockSpec((tm, tn), lambda j, i, te, nv: (i, j)),
            scratch_shapes=[pltpu.VMEM((d, tn), BF16), pltpu.VMEM((d, tn), BF16)],
        ),
        compiler_params=_cparams("arbitrary", "arbitrary"),
        name="moe_gate_up",
    )(tile_expert, n_valid, xs, w_gu_l, w_gu_l)


def _moe_dn_kernel(te_ref, nv_ref, h_ref, w_ref, o_ref, wb):
    i = pl.program_id(0)
    valid = i < nv_ref[0]

    @pl.when(valid & _new_expert(te_ref, i))
    def _():
        wb[...] = w_ref[...].astype(BF16)

    @pl.when(valid)
    def _():
        o_ref[...] = jnp.dot(h_ref[...], wb[...], preferred_element_type=F32)

    @pl.when(jnp.logical_not(valid))
    def _():
        o_ref[...] = jnp.zeros_like(o_ref)


def _moe_down(hmid, w_dn_l, tile_expert, n_valid):
    p, de = hmid.shape
    d = w_dn_l.shape[2]
    tm = MOE_TM
    row = lambda i, nv: jnp.minimum(i, nv[0] - 1)
    return pl.pallas_call(
        _moe_dn_kernel,
        out_shape=jax.ShapeDtypeStruct((p, d), F32),
        grid_spec=pltpu.PrefetchScalarGridSpec(
            num_scalar_prefetch=2,
            grid=(p // tm,),
            in_specs=[
                pl.BlockSpec((tm, de), lambda i, te, nv: (row(i, nv), 0)),
                pl.BlockSpec((None, de, d), lambda i, te, nv: (te[i], 0, 0)),
            ],
            out_specs=pl.BlockSpec((tm, d), lambda i, te, nv: (i, 0)),
            scratch_shapes=[pltpu.VMEM((de, d), BF16)],
        ),
        compiler_params=_cparams("arbitrary"),
        name="moe_down",
    )(tile_expert, n_valid, hmid, w_dn_l)


def _moe_combine_kernel(p1_ref, p2_ref, y_hbm, x_ref, w_ref, g_ref, o_ref, buf1, buf2, sem):
    i = pl.program_id(0)
    tm = buf1.shape[1]
    slot = i % 2

    def issue_tile(tile, s):
        base = tile * tm

        def issue(r, c):
            pltpu.make_async_copy(y_hbm.at[pl.ds(p1_ref[base + r], 1), :], buf1.at[s, pl.ds(r, 1), :], sem.at[s]).start()
            pltpu.make_async_copy(y_hbm.at[pl.ds(p2_ref[base + r], 1), :], buf2.at[s, pl.ds(r, 1), :], sem.at[s]).start()
            return c

        lax.fori_loop(0, tm, issue, 0, unroll=8)

    @pl.when(i == 0)
    def _():
        issue_tile(0, 0)

    @pl.when(i + 1 < pl.num_programs(0))
    def _():
        issue_tile(i + 1, 1 - slot)

    pltpu.make_async_copy(y_hbm.at[pl.ds(0, tm), :], buf1.at[slot], sem.at[slot]).wait()
    pltpu.make_async_copy(y_hbm.at[pl.ds(0, tm), :], buf2.at[slot], sem.at[slot]).wait()
    w = w_ref[...]
    y = w[:, 0:1] * buf1[slot] + w[:, 1:2] * buf2[slot]
    o_ref[...] = x_ref[...] + g_ref[...] * y


def _moe_combine(y_sorted, x, wts, p1, p2, mods3, layer, lay):
    t, d = x.shape
    tm = ROW_TILE
    return pl.pallas_call(
        _moe_combine_kernel,
        out_shape=jax.ShapeDtypeStruct((t, d), F32),
        grid_spec=pltpu.PrefetchScalarGridSpec(
            num_scalar_prefetch=2,
            grid=(t // tm,),
            in_specs=[
                pl.BlockSpec(memory_space=pl.ANY),
                pl.BlockSpec((tm, d), lambda i, p1, p2: (i, 0)),
                pl.BlockSpec((tm, wts.shape[1]), lambda i, p1, p2: (i, 0)),
                pl.BlockSpec((None, 1, d), lambda i, p1, p2: (_mod_index(layer, 5, lay.mod_row(i, tm)), 0, 0)),
            ],
            out_specs=pl.BlockSpec((tm, d), lambda i, p1, p2: (i, 0)),
            scratch_shapes=[pltpu.VMEM((2, tm, d), F32), pltpu.VMEM((2, tm, d), F32), pltpu.SemaphoreType.DMA((2,))],
        ),
        compiler_params=_cparams("arbitrary"),
        name="moe_combine",
    )(p1, p2, y_sorted, x, wts, mods3)


def _moe_plan(route, n_experts):
    t = route.shape[1]
    tm = MOE_TM
    eid = route[0:2].astype(jnp.int32).reshape(-1)
    onehot = (eid[:, None] == jnp.arange(n_experts, dtype=jnp.int32)[None, :]).astype(jnp.int32)
    csum = jnp.cumsum(onehot, axis=0)
    rank = jnp.take_along_axis(csum, eid[:, None], axis=1)[:, 0] - 1
    counts = csum[-1]
    padded = ((counts + tm - 1) // tm) * tm
    ends = jnp.cumsum(padded)
    offs = ends - padded
    pos = offs[eid] + rank
    n_rows = 2 * t + n_experts * tm
    tok = jnp.tile(jnp.arange(t, dtype=jnp.int32), 2)
    src = jnp.zeros((n_rows,), jnp.int32).at[pos].set(tok)
    n_tiles = n_rows // tm
    n_valid = (ends[-1] // tm).astype(jnp.int32)
    starts = jnp.arange(n_tiles, dtype=jnp.int32) * tm
    te = jnp.sum((starts[:, None] >= ends[None, :]).astype(jnp.int32), axis=1)
    te_last = jnp.sum((((n_valid - 1) * tm) >= ends).astype(jnp.int32))
    te = jnp.where(starts < ends[-1], te, te_last).astype(jnp.int32)
    wts = jnp.transpose(route[2:4])
    wts = jnp.pad(wts, ((0, 0), (0, 6)))
    return src, te, n_valid.reshape(1), pos[:t], pos[t:], wts, n_rows


def _moe_layer(x, h2, route, mods3, layer, lay, w_gu_l, w_dn_l):
    n_experts = w_gu_l.shape[0]
    src, te, n_valid, p1, p2, wts, n_rows = _moe_plan(route, n_experts)
    xs = _moe_gather(h2, src, n_valid, n_rows)
    hmid = _moe_gate_up(xs, w_gu_l, te, n_valid)
    ys = _moe_down(hmid, w_dn_l, te, n_valid)
    return _moe_combine(ys, x, wts, p1, p2, mods3, layer, lay)


def kernel(x_prompt, x_sample, state_s5_re, state_s5_im, state_lru, cache_attn_k, cache_attn_v, c, c_ctx, w_ada, b_ada, norm1_g, norm2_g, final_norm_g, w_in_even, w_out_even, s5_lam_re, s5_lam_im, s5_log_dt, s5_b_re, s5_b_im, s5_c_re, s5_c_im, s5_d, s5_w_glu, s5_b_glu, lru_conv_w, lru_conv_b, lru_w_a, lru_b_a, lru_w_x, lru_b_x, lru_lam, w_qkv, w_o, rpb, w_router, b_router, w_gate_up, w_down):
    batch, seq, d = x_prompt.shape
    dec_batch, dec_seq, _ = x_sample.shape
    depth = w_ada.shape[0]
    n_heads = cache_attn_k.shape[3]
    s5_w = s5_d.shape[1]
    lru_w = lru_conv_w.shape[2]
    n_groups, n_state = s5_lam_re.shape[2], s5_lam_re.shape[3]
    assert dec_batch < MOD_ROWS
    n_ctx = batch * seq
    assert n_ctx % MM_TM == 0 and dec_seq % MM_TM == 0 and seq % ROW_TILE == 0 and dec_seq % ROW_TILE == 0
    lay = _Layout(n_ctx, dec_batch, dec_seq)
    t = lay.total

    x = jnp.concatenate([x_prompt.reshape(n_ctx, d), x_sample.reshape(dec_batch * dec_seq, d)], axis=0)
    cvec = jnp.zeros((MOD_ROWS, d), F32).at[:dec_batch].set(c).at[dec_batch].set(c_ctx)
    mods = _ada_project(cvec, w_ada, b_ada)
    mods3 = mods.reshape(depth * MOD_ROWS * N_MOD, 1, d)
    w_router_t = jnp.transpose(w_router)
    b_router_col = b_router.reshape(-1, 1)

    s5_re_list, s5_im_list, lru_list = [], [], []
    kv_caches = None
    for l in range(depth):
        j = l // 2
        h1 = _norm_mod(x, norm1_g[l].reshape(1, d), mods3, l, 0, lay, BF16)
        if l % 2 == 0:
            proj = _matmul([h1], w_in_even[j], F32)
            mats = _s5_matrices(s5_lam_re[j], s5_lam_im[j], s5_log_dt[j], s5_b_re[j], s5_b_im[j],
                                s5_c_re[j], s5_c_im[j])
            u = proj[:, :s5_w]
            zero_h0 = jnp.zeros((batch, 2, n_groups, 2 * n_state), F32)
            lat_h0 = jnp.concatenate([state_s5_re[:, j], state_s5_im[:, j]], axis=-1).astype(F32)
            y_ctx, s5_fin = _s5_scan(u[:n_ctx], zero_h0, mats, batch, seq)
            y_lat, _ = _s5_scan(u[n_ctx:], lat_h0, mats, dec_batch, dec_seq)
            s5_re_list.append(s5_fin[..., :n_state])
            s5_im_list.append(s5_fin[..., n_state:])
            y_s5 = _s5_glu(jnp.concatenate([y_ctx, y_lat], axis=0), proj, s5_d[j].reshape(1, s5_w),
                           s5_w_glu[j], s5_b_glu[j].reshape(1, s5_w))
            wa_bd = _block_diag_heads(lru_w_a[j])
            wx_bd = _block_diag_heads(lru_w_x[j])
            sp = jax.nn.softplus(-lru_lam[j].astype(F32))
            lru_args = (lru_conv_w[j], lru_conv_b[j].reshape(1, lru_w), wa_bd, wx_bd, lru_b_a[j], lru_b_x[j], sp)
            y_lru_ctx, lru_fin = _lru_mixer(proj, 0, batch, seq, lru_w, jnp.zeros((batch, 2, lru_w), F32), *lru_args)
            y_lru_lat, _ = _lru_mixer(proj, n_ctx, dec_batch, dec_seq, lru_w // 2,
                                      state_lru[:, j].astype(F32), *lru_args)
            lru_list.append(lru_fin)
            y_lru = jnp.concatenate([y_lru_ctx, y_lru_lat], axis=0)
            x = _matmul([y_s5, y_lru], w_out_even[j], F32, resid=(x, mods3, l, 2, lay))
        else:
            qkv = _matmul([h1], w_qkv[j], F32)
            o_all, *kv_caches = _ctx_attention(qkv, batch, seq, n_heads, t, j, depth // 2, kv_caches)
            bias = _nbr_bias(rpb[j], dec_seq // GRID_W)
            o_all = _nbr_attention(qkv, o_all, cache_attn_k, cache_attn_v, j, bias, n_ctx, dec_batch, dec_seq, n_heads)
            x = _matmul([o_all], w_o[j], F32, resid=(x, mods3, l, 2, lay))
        h2, route = _norm_mod_route(x, norm2_g[l].reshape(1, d), mods3, l, lay, w_router_t, b_router_col)
        x = _moe_layer(x, h2, route, mods3, l, lay, w_gate_up[l], w_down[l])

    g_fin = final_norm_g.reshape(1, d)
    y_prompt = _final_norm(x, g_fin, 0, n_ctx).reshape(batch, seq, d)
    y_sample = _final_norm(x, g_fin, n_ctx, t - n_ctx).reshape(dec_batch, dec_seq, d)
    cache_shape = (batch, depth // 2, seq, n_heads, d // n_heads)
    return (y_prompt, y_sample, jnp.stack(s5_re_list, axis=1), jnp.stack(s5_im_list, axis=1),
            jnp.stack(lru_list, axis=1), kv_caches[0].reshape(cache_shape), kv_caches[1].reshape(cache_shape))
```

```python
import functools
import math

import numpy as np
import jax
import jax.numpy as jnp
from jax import lax
from jax.experimental import pallas as pl
from jax.experimental.pallas import tpu as pltpu

F32 = jnp.float32
BF16 = jnp.bfloat16

LANES = 128
NORM_EPS = 1e-6
NEG_INF = -1e30
S5_GROUP = 16
S5_CHUNK = 16
S5_LANE_TILE = 256
S5_ROW_BLOCK = 2048
LRU_C = 8.0
LRU_HEAD_BLOCK = 256
N_EXPERT_GROUPS = 4
WIN_ROWS_MAX = 8
WIN_COLS = 16
GRID_W = 64
N_MOD = 6
MOD_ROWS = 8
VMEM_LIMIT = 52 * 1024 * 1024
ROW_TILE = 256
MM_TM = 1024
MM_TN = 1024
MM_TN_RESID = 512
MOE_TM = 512
MOE_TN = 512


def _cparams(*sem):
    return pltpu.CompilerParams(dimension_semantics=sem, vmem_limit_bytes=VMEM_LIMIT)


def _gelu(x):
    return 0.5 * x * (1.0 + jnp.tanh(math.sqrt(2.0 / math.pi) * (x + 0.044715 * (x * x * x))))


class _Layout:
    def __init__(self, n_ctx_rows, dec_batch, dec_seq):
        self.n_ctx = n_ctx_rows
        self.dec_batch = dec_batch
        self.dec_seq = dec_seq
        self.total = n_ctx_rows + dec_batch * dec_seq

    def mod_row(self, i, tm):
        nct = self.n_ctx // tm
        per = self.dec_seq // tm
        return jnp.where(i < nct, self.dec_batch, (i - nct) // per)


def _mod_index(layer, which, row):
    return (layer * MOD_ROWS + row) * N_MOD + which


def _ada_kernel(c_ref, w_ref, b_ref, o_ref):
    c = c_ref[...]
    s = (c * jax.nn.sigmoid(c)).astype(BF16)
    o_ref[...] = jnp.dot(s, w_ref[...].astype(BF16), preferred_element_type=F32) + b_ref[...]


def _ada_project(cvec, w_ada, b_ada):
    depth, d, n = w_ada.shape
    tn = MM_TN
    return pl.pallas_call(
        _ada_kernel,
        out_shape=jax.ShapeDtypeStruct((depth, MOD_ROWS, n), F32),
        grid=(depth, n // tn),
        in_specs=[
            pl.BlockSpec((MOD_ROWS, d), lambda l, j: (0, 0)),
            pl.BlockSpec((None, d, tn), lambda l, j: (l, 0, j)),
            pl.BlockSpec((None, 1, tn), lambda l, j: (l, 0, j)),
        ],
        out_specs=pl.BlockSpec((None, MOD_ROWS, tn), lambda l, j: (l, 0, j)),
        compiler_params=_cparams("arbitrary", "arbitrary"),
        name="ada_project",
    )(cvec, w_ada, b_ada.reshape(depth, 1, n))


def _rms(x, g):
    ms = jnp.mean(x * x, axis=-1, keepdims=True)
    return x * lax.rsqrt(ms + NORM_EPS) * g


def _norm_mod_kernel(x_ref, g_ref, sh_ref, sc_ref, o_ref):
    y = _rms(x_ref[...], g_ref[...])
    o_ref[...] = (y * (1.0 + sc_ref[...]) + sh_ref[...]).astype(o_ref.dtype)


def _norm_mod(x, g_row, mods3, layer, which_shift, lay, out_dtype):
    t, d = x.shape
    tm = ROW_TILE
    row = lambda i: lay.mod_row(i, tm)
    return pl.pallas_call(
        _norm_mod_kernel,
        out_shape=jax.ShapeDtypeStruct((t, d), out_dtype),
        grid=(t // tm,),
        in_specs=[
            pl.BlockSpec((tm, d), lambda i: (i, 0)),
            pl.BlockSpec((1, d), lambda i: (0, 0)),
            pl.BlockSpec((None, 1, d), lambda i: (_mod_index(layer, which_shift, row(i)), 0, 0)),
            pl.BlockSpec((None, 1, d), lambda i: (_mod_index(layer, which_shift + 1, row(i)), 0, 0)),
        ],
        out_specs=pl.BlockSpec((tm, d), lambda i: (i, 0)),
        compiler_params=_cparams("arbitrary"),
        name="norm_mod",
    )(x, g_row, mods3, mods3)


def _store_slabs(slab_ref, x):
    rows, d = x.shape
    nk = d // LANES
    for k in range(nk):
        slab_ref[pl.ds(k, rows, stride=nk), :] = x[:, k * LANES:(k + 1) * LANES].astype(slab_ref.dtype)


def _top2_of4(a):
    m1 = jnp.maximum(jnp.maximum(a[0], a[1]), jnp.maximum(a[2], a[3]))
    i1 = jnp.where(a[0] == m1, 0, jnp.where(a[1] == m1, 1, jnp.where(a[2] == m1, 2, 3)))
    b = [jnp.where(i1 == k, -jnp.inf, a[k]) for k in range(4)]
    m2 = jnp.maximum(jnp.maximum(b[0], b[1]), jnp.maximum(b[2], b[3]))
    i2 = jnp.where(b[0] == m2, 0, jnp.where(b[1] == m2, 1, jnp.where(b[2] == m2, 2, 3)))
    return m1 + m2, i1, i2


def _norm_mod_route_kernel(x_ref, g_ref, sh_ref, sc_ref, wr_ref, br_ref, o_ref, r_ref):
    y = _rms(x_ref[...], g_ref[...])
    h = y * (1.0 + sc_ref[...]) + sh_ref[...]
    _store_slabs(o_ref, h)
    logits = lax.dot_general(wr_ref[...], h, (((1,), (1,)), ((), ())),
                             precision=lax.Precision.HIGHEST, preferred_element_type=F32)
    scores = jax.nn.sigmoid(logits)
    sel = scores + br_ref[...]
    n_e = scores.shape[0]
    per = n_e // N_EXPERT_GROUPS
    sel_rows = [sel[e:e + 1, :] for e in range(n_e)]
    score_rows = [scores[e:e + 1, :] for e in range(n_e)]
    gs, i1s, i2s = [], [], []
    for gi in range(N_EXPERT_GROUPS):
        s, i1, i2 = _top2_of4(sel_rows[gi * per:(gi + 1) * per])
        gs.append(s)
        i1s.append(i1)
        i2s.append(i2)
    gmax = jnp.maximum(jnp.maximum(gs[0], gs[1]), jnp.maximum(gs[2], gs[3]))
    gsel = jnp.where(gs[0] == gmax, 0, jnp.where(gs[1] == gmax, 1, jnp.where(gs[2] == gmax, 2, 3)))
    l1 = jnp.where(gsel == 0, i1s[0], jnp.where(gsel == 1, i1s[1], jnp.where(gsel == 2, i1s[2], i1s[3])))
    l2 = jnp.where(gsel == 0, i2s[0], jnp.where(gsel == 1, i2s[1], jnp.where(gsel == 2, i2s[2], i2s[3])))
    e1 = gsel * per + l1
    e2 = gsel * per + l2
    w1 = jnp.zeros_like(gmax)
    w2 = jnp.zeros_like(gmax)
    for e in range(n_e):
        w1 = jnp.where(e1 == e, score_rows[e], w1)
        w2 = jnp.where(e2 == e, score_rows[e], w2)
    wsum = w1 + w2
    zero = jnp.zeros_like(gmax)
    r_ref[...] = jnp.concatenate(
        [e1.astype(F32), e2.astype(F32), w1 / wsum, w2 / wsum, zero, zero, zero, zero], axis=0)


def _norm_mod_route(x, g_row, mods3, layer, lay, w_router_t, b_router_col):
    t, d = x.shape
    tm = ROW_TILE
    n_e = w_router_t.shape[0]
    row = lambda i: lay.mod_row(i, tm)
    return pl.pallas_call(
        _norm_mod_route_kernel,
        out_shape=(jax.ShapeDtypeStruct((t * (d // LANES), LANES), F32), jax.ShapeDtypeStruct((8, t), F32)),
        grid=(t // tm,),
        in_specs=[
            pl.BlockSpec((tm, d), lambda i: (i, 0)),
            pl.BlockSpec((1, d), lambda i: (0, 0)),
            pl.BlockSpec((None, 1, d), lambda i: (_mod_index(layer, 3, row(i)), 0, 0)),
            pl.BlockSpec((None, 1, d), lambda i: (_mod_index(layer, 4, row(i)), 0, 0)),
            pl.BlockSpec((n_e, d), lambda i: (0, 0)),
            pl.BlockSpec((n_e, 1), lambda i: (0, 0)),
        ],
        out_specs=(pl.BlockSpec((tm * (d // LANES), LANES), lambda i: (i, 0)),
                   pl.BlockSpec((8, tm), lambda i: (0, i))),
        compiler_params=_cparams("arbitrary"),
        name="norm_mod_route",
    )(x, g_row, mods3, mods3, w_router_t, b_router_col)


def _final_norm_kernel(x_ref, g_ref, o_ref):
    o_ref[...] = _rms(x_ref[...], g_ref[...])


def _final_norm(x, g_row, row0, n_rows):
    d = x.shape[1]
    tm = ROW_TILE
    return pl.pallas_call(
        _final_norm_kernel,
        out_shape=jax.ShapeDtypeStruct((n_rows, d), F32),
        grid=(n_rows // tm,),
        in_specs=[pl.BlockSpec((tm, d), lambda i: (row0 // tm + i, 0)), pl.BlockSpec((1, d), lambda i: (0, 0))],
        out_specs=pl.BlockSpec((tm, d), lambda i: (i, 0)),
        compiler_params=_cparams("arbitrary"),
        name="final_norm",
    )(x, g_row)


def _mm_kernel(*refs, n_a, resid):
    a_refs = refs[:n_a]
    w_ref = refs[n_a]
    pos = n_a + 1
    if resid:
        x_ref, g_ref = refs[pos], refs[pos + 1]
        pos += 2
    o_ref, wb_ref = refs[pos], refs[pos + 1]

    @pl.when(pl.program_id(1) == 0)
    def _():
        wb_ref[...] = w_ref[...].astype(BF16)

    acc = None
    k0 = 0
    for a_ref in a_refs:
        ka = a_ref.shape[1]
        part = jnp.dot(a_ref[...], wb_ref[k0:k0 + ka, :], preferred_element_type=F32)
        acc = part if acc is None else acc + part
        k0 += ka
    if resid:
        o_ref[...] = x_ref[...] + g_ref[...] * acc
    else:
        o_ref[...] = acc.astype(o_ref.dtype)


def _matmul(a_list, w, w_layer, out_dtype, resid=None):
    m = a_list[0].shape[0]
    _, k, n = w.shape
    tm = MM_TM
    tn = min(MM_TN_RESID if resid is not None else MM_TN, n)
    in_specs = [pl.BlockSpec((tm, a.shape[1]), lambda j, i: (i, 0)) for a in a_list]
    in_specs.append(pl.BlockSpec((None, k, tn), lambda j, i: (w_layer, 0, j)))
    args = list(a_list) + [w]
    if resid is not None:
        x, mods3, layer, which, lay = resid
        in_specs.append(pl.BlockSpec((tm, tn), lambda j, i: (i, j)))
        in_specs.append(pl.BlockSpec(
            (None, 1, tn), lambda j, i: (_mod_index(layer, which, lay.mod_row(i, tm)), 0, j)))
        args += [x, mods3]
    return pl.pallas_call(
        functools.partial(_mm_kernel, n_a=len(a_list), resid=resid is not None),
        out_shape=jax.ShapeDtypeStruct((m, n), out_dtype),
        grid=(n // tn, m // tm),
        in_specs=in_specs,
        out_specs=pl.BlockSpec((tm, tn), lambda j, i: (i, j)),
        scratch_shapes=[pltpu.VMEM((k, tn), BF16)],
        compiler_params=_cparams("arbitrary", "arbitrary"),
        name="matmul_resid" if resid is not None else "matmul",
    )(*args)


def _cmul(ar, ai, br, bi):
    return ar * br - ai * bi, ar * bi + ai * br


def _s5_matrices(lam_re, lam_im, log_dt, b_re, b_im, c_re, c_im):
    s = S5_CHUNK
    f = lambda z: z.astype(F32)
    lam_re, lam_im, log_dt, b_re, b_im, c_re, c_im = map(f, (lam_re, lam_im, log_dt, b_re, b_im, c_re, c_im))
    dt = jnp.exp(log_dt)[..., None]
    kk = jnp.arange(s + 1, dtype=F32)[:, None, None, None]
    mag = jnp.exp(kk * (lam_re * dt)[None])
    ph = kk * (lam_im * dt)[None]
    pw_re, pw_im = mag * jnp.cos(ph), mag * jnp.sin(ph)
    a_re, a_im = pw_re[1], pw_im[1]
    den = lam_re * lam_re + lam_im * lam_im
    q_re, q_im = _cmul(a_re - 1.0, a_im, lam_re / den, -lam_im / den)
    bb_re, bb_im = _cmul(q_re[..., None], q_im[..., None], b_re, b_im)
    cp_re, cp_im = _cmul(c_re[:, :, None], c_im[:, :, None],
                         jnp.moveaxis(pw_re, 0, 2)[:, :, :, None, :], jnp.moveaxis(pw_im, 0, 2)[:, :, :, None, :])
    bt_re = jnp.swapaxes(bb_re, 2, 3)[:, :, None, None]
    bt_im = jnp.swapaxes(bb_im, 2, 3)[:, :, None, None]
    kern = jnp.sum(cp_re[:, :, :, :, None, :] * bt_re - cp_im[:, :, :, :, None, :] * bt_im, axis=-1)
    sp = np.arange(s)[:, None]
    so = np.arange(s)[None, :]
    lag_f = np.clip(so - sp, 0, s)
    lag_b = np.clip(sp - so, 0, s)
    kf = jnp.where(jnp.asarray(so >= sp)[None, :, :, None, None], kern[0][:, lag_f], 0.0)
    kb = jnp.where(jnp.asarray(sp >= so)[None, :, :, None, None], kern[1][:, lag_b], 0.0)
    g = kern.shape[1]
    n = s * S5_GROUP
    tmat = jnp.transpose(kf + kb, (0, 1, 4, 2, 3)).reshape(g, n, n)

    def e_mat(d, powers):
        pr = jnp.moveaxis(pw_re[powers, d], 0, 1)[:, :, :, None]
        pi = jnp.moveaxis(pw_im[powers, d], 0, 1)[:, :, :, None]
        er, ei = _cmul(pr, pi, bb_re[d][:, None], bb_im[d][:, None])
        er = jnp.transpose(er, (0, 1, 3, 2)).reshape(g, n, -1)
        ei = jnp.transpose(ei, (0, 1, 3, 2)).reshape(g, n, -1)
        return jnp.concatenate([er, ei], axis=-1)

    def c_mat(d, powers):
        cr = jnp.transpose(cp_re[d][:, powers], (0, 3, 1, 2)).reshape(g, -1, n)
        ci = jnp.transpose(cp_im[d][:, powers], (0, 3, 1, 2)).reshape(g, -1, n)
        return jnp.concatenate([cr, -ci], axis=1)

    swap = lambda e: jnp.concatenate([e[..., e.shape[-1] // 2:], e[..., :e.shape[-1] // 2]], axis=-1)
    e_f = e_mat(0, np.arange(s - 1, -1, -1))
    e_b = e_mat(1, np.arange(s))
    e_all = jnp.concatenate([e_f, swap(e_f), e_b, swap(e_b)], axis=-1)
    c_all = jnp.concatenate([c_mat(0, np.arange(1, s + 1)), c_mat(1, np.arange(s, 0, -1))], axis=1)
    dec_r = jnp.concatenate([pw_re[s], pw_re[s]], axis=-1)
    dec_i = jnp.concatenate([-pw_im[s], pw_im[s]], axis=-1)
    decay = jnp.stack([dec_r, dec_i], axis=2)[:, :, :, None, :]
    return tmat.astype(BF16), e_all.astype(BF16), c_all.astype(BF16), decay


def _block_transpose(v):
    n, w = v.shape
    rows = lax.broadcasted_iota(jnp.int32, v.shape, 0)
    lanes = lax.broadcasted_iota(jnp.int32, v.shape, 1)
    k = S5_CHUNK // 2
    while k >= 1:
        up = pltpu.roll(pltpu.roll(v, n - k, 0), S5_GROUP * k, 1)
        dn = pltpu.roll(pltpu.roll(v, k, 0), w - S5_GROUP * k, 1)
        rbit = (rows & k) != 0
        gbit = (lanes & (S5_GROUP * k)) != 0
        v = jnp.where(jnp.logical_and(jnp.logical_not(rbit), gbit), up,
                      jnp.where(jnp.logical_and(rbit, jnp.logical_not(gbit)), dn, v))
        k //= 2
    return v


def _s5_kernel(x_ref, t_ref, e_ref, c_ref, dec_ref, h0_ref, *rest, nbk, nc):
    y_ref, hfin_ref, v0, v1, w0, w1, zf, zfs, zb, zbs, hsf, hsb = rest[-12:]
    lw = x_ref.shape[1]
    gpt = lw // S5_GROUP
    nr = nbk * nc
    p2 = hsf.shape[1]
    v = _block_transpose(x_ref[...])
    v0[...] = v[:, :LANES]
    v1[...] = v[:, LANES:]

    def group(g, carry):
        rows_g = pl.ds(g, nr, stride=gpt)
        u = jnp.concatenate([v0[rows_g, :], v1[rows_g, :]], axis=1).astype(BF16)
        z = jnp.dot(u, e_ref[g], preferred_element_type=F32)
        zf[...] = z[:, 0:p2]
        zfs[...] = z[:, p2:2 * p2]
        zb[...] = z[:, 2 * p2:3 * p2]
        zbs[...] = z[:, 3 * p2:4 * p2]
        ar_f, ai_f, ar_b, ai_b = dec_ref[0, g, 0], dec_ref[0, g, 1], dec_ref[1, g, 0], dec_ref[1, g, 1]
        hf, hfs, hb, hbs = h0_ref[g, 0], h0_ref[g, 1], h0_ref[g, 2], h0_ref[g, 3]
        for c in range(nc):
            rf = pl.ds(c, nbk, stride=nc)
            rb = pl.ds(nc - 1 - c, nbk, stride=nc)
            hsf[rf, :] = hf
            hsb[rb, :] = hb
            hf, hfs = ar_f * hf + ai_f * hfs + zf[rf, :], ar_f * hfs - ai_f * hf + zfs[rf, :]
            hb, hbs = ar_b * hb + ai_b * hbs + zb[rb, :], ar_b * hbs - ai_b * hb + zbs[rb, :]
        hfin_ref[g, 0] = hf
        hfin_ref[g, 1] = hb
        hs = jnp.concatenate([hsf[...], hsb[...]], axis=1).astype(BF16)
        y = jnp.dot(u, t_ref[g], preferred_element_type=F32)
        y = y + jnp.dot(hs, c_ref[g], preferred_element_type=F32)
        w0[rows_g, :] = y[:, :LANES]
        w1[rows_g, :] = y[:, LANES:]
        return carry

    lax.fori_loop(0, gpt, group, 0)
    y_ref[...] = _block_transpose(jnp.concatenate([w0[...], w1[...]], axis=1))


def _s5_scan(proj, row0, h0, mats, n_seq, seq_len, width, y_prev):
    tmat, e_all, c_all, decay = mats
    n = tmat.shape[1]
    p2 = decay.shape[-1]
    lw = S5_LANE_TILE
    gpt = lw // S5_GROUP
    nc = seq_len // S5_CHUNK
    nbk = max(1, S5_ROW_BLOCK // seq_len)
    rb_rows = nbk * seq_len
    n_rb = n_seq // nbk
    nr = nbk * nc
    rb0 = row0 // rb_rows
    assert n_seq % nbk == 0 and row0 % rb_rows == 0 and p2 == LANES and lw == 2 * LANES
    g = tmat.shape[0]
    h0s = jnp.concatenate([h0[..., p2 // 2:], h0[..., :p2 // 2]], axis=-1)
    h04 = jnp.stack([h0[:, 0], h0s[:, 0], h0[:, 1], h0s[:, 1]], axis=1)
    h0p = jnp.transpose(h04.reshape(n_rb, nbk, 4, g, p2), (0, 3, 2, 1, 4))
    wspec = lambda r, c: pl.BlockSpec((gpt, r, c), lambda lt, rb: (lt, 0, 0))
    in_specs = [
        pl.BlockSpec((rb_rows, lw), lambda lt, rb: (rb0 + rb, lt)),
        wspec(n, n), wspec(n, 4 * p2), wspec(2 * p2, n),
        pl.BlockSpec((2, gpt, 2, 1, p2), lambda lt, rb: (0, lt, 0, 0, 0)),
        pl.BlockSpec((None, gpt, 4, nbk, p2), lambda lt, rb: (rb, lt, 0, 0, 0)),
    ]
    args = [proj, tmat, e_all, c_all, decay, h0p]
    aliases = {}
    if y_prev is not None:
        in_specs.append(pl.BlockSpec(memory_space=pl.ANY))
        args.append(y_prev)
        aliases = {6: 0}
    y, hfin = pl.pallas_call(
        functools.partial(_s5_kernel, nbk=nbk, nc=nc),
        out_shape=(jax.ShapeDtypeStruct((proj.shape[0], width), F32),
                   jax.ShapeDtypeStruct((n_rb, g, 2, nbk, p2), F32)),
        grid=(width // lw, n_rb),
        in_specs=in_specs,
        out_specs=(pl.BlockSpec((rb_rows, lw), lambda lt, rb: (rb0 + rb, lt)),
                   pl.BlockSpec((None, gpt, 2, nbk, p2), lambda lt, rb: (rb, lt, 0, 0, 0))),
        scratch_shapes=[pltpu.VMEM((rb_rows, LANES), F32)] * 4 + [pltpu.VMEM((nr, p2), F32)] * 6,
        input_output_aliases=aliases,
        compiler_params=_cparams("arbitrary", "arbitrary"),
        name="s5_scan",
    )(*args)
    fin = jnp.transpose(hfin, (0, 3, 2, 1, 4)).reshape(n_seq, 2, g, p2)
    return y, fin


def _s5_glu_kernel(y_ref, u_ref, d_ref, w_ref, b_ref, o_ref, wb_ref):
    @pl.when(pl.program_id(0) == 0)
    def _():
        wb_ref[...] = w_ref[...].astype(BF16)

    z = _gelu(y_ref[...] + d_ref[...] * u_ref[...])
    gate = jnp.dot(z.astype(BF16), wb_ref[...], preferred_element_type=F32) + b_ref[...]
    o_ref[...] = (z * jax.nn.sigmoid(gate)).astype(o_ref.dtype)


def _s5_glu(y, proj, d_row, w_glu, w_layer, b_row):
    t, w = y.shape
    tm = MM_TM
    return pl.pallas_call(
        _s5_glu_kernel,
        out_shape=jax.ShapeDtypeStruct((t, w), BF16),
        grid=(t // tm,),
        in_specs=[
            pl.BlockSpec((tm, w), lambda i: (i, 0)),
            pl.BlockSpec((tm, w), lambda i: (i, 0)),
            pl.BlockSpec((1, w), lambda i: (0, 0)),
            pl.BlockSpec((None, w, w), lambda i: (w_layer, 0, 0)),
            pl.BlockSpec((1, w), lambda i: (0, 0)),
        ],
        out_specs=pl.BlockSpec((tm, w), lambda i: (i, 0)),
        scratch_shapes=[pltpu.VMEM((w, w), BF16)],
        compiler_params=_cparams("arbitrary"),
        name="s5_glu",
    )(y, proj, d_row, w_glu, b_row)


def _lru_kernel(gate_ref, xr_ref, cw_ref, cb_ref, wa_ref, wx_ref, ba_ref, bx_ref, sp_ref, h0_ref, *rest):
    y_ref, hfin_ref, a_f, b_f, a_b, b_b = rest[-6:]
    seq, lw = xr_ref.shape
    x = xr_ref[...]
    rows = lax.broadcasted_iota(jnp.int32, (seq, lw), 0)
    cw = cw_ref[...]
    xc = cw[2:3] * x + cb_ref[...]
    xc = xc + cw[0:1] * jnp.where(rows >= 2, pltpu.roll(x, 2, 0), 0.0)
    xc = xc + cw[1:2] * jnp.where(rows >= 1, pltpu.roll(x, 1, 0), 0.0)
    xc = xc + cw[3:4] * jnp.where(rows < seq - 1, pltpu.roll(x, seq - 1, 0), 0.0)
    for hb in range(lw // LRU_HEAD_BLOCK):
        lanes = slice(hb * LRU_HEAD_BLOCK, (hb + 1) * LRU_HEAD_BLOCK)
        xb = xc[:, lanes]
        xbb = xb.astype(BF16)
        for d, (a_s, b_s) in enumerate(((a_f, b_f), (a_b, b_b))):
            r = jax.nn.sigmoid(jnp.dot(xbb, wa_ref[d, hb], preferred_element_type=F32) + ba_ref[d:d + 1, lanes])
            gi = jax.nn.sigmoid(jnp.dot(xbb, wx_ref[d, hb], preferred_element_type=F32) + bx_ref[d:d + 1, lanes])
            a = jnp.exp(-LRU_C * r * sp_ref[d:d + 1, lanes])
            a_s[:, lanes] = a
            b_s[:, lanes] = jnp.sqrt(1.0 - a * a) * (gi * xb)

    def step(t, carry):
        hf, hb = carry
        rf = pl.ds(t, 1)
        hf = a_f[rf, :] * hf + b_f[rf, :]
        b_f[rf, :] = hf
        rb = pl.ds(seq - 1 - t, 1)
        hb = a_b[rb, :] * hb + b_b[rb, :]
        b_b[rb, :] = hb
        return hf, hb

    hf, hb = lax.fori_loop(0, seq, step, (h0_ref[0:1, :], h0_ref[1:2, :]), unroll=8)
    hfin_ref[0:1, :] = hf
    hfin_ref[1:2, :] = hb
    y_ref[...] = (_gelu(gate_ref[...]) * (b_f[...] + b_b[...])).astype(y_ref.dtype)


def _lru_mixer(proj, row0, n_seq, seq_len, lane_w, h0, conv_w, conv_b, wa_bd, wx_bd, b_a, b_x, sp, y_prev):
    w = conv_w.shape[1]
    nlb = w // lane_w
    rb0 = row0 // seq_len
    hpb = lane_w // LRU_HEAD_BLOCK
    extra_specs, extra_args, aliases = [], [], {}
    if y_prev is not None:
        extra_specs, extra_args, aliases = [pl.BlockSpec(memory_space=pl.ANY)], [y_prev], {10: 0}
    return pl.pallas_call(
        _lru_kernel,
        out_shape=(jax.ShapeDtypeStruct((proj.shape[0], w), BF16), jax.ShapeDtypeStruct((n_seq, 2, w), F32)),
        grid=(n_seq, nlb),
        input_output_aliases=aliases,
        in_specs=[
            pl.BlockSpec((seq_len, lane_w), lambda b, c: (rb0 + b, nlb + c)),
            pl.BlockSpec((seq_len, lane_w), lambda b, c: (rb0 + b, 2 * nlb + c)),
            pl.BlockSpec((conv_w.shape[0], lane_w), lambda b, c: (0, c)),
            pl.BlockSpec((1, lane_w), lambda b, c: (0, c)),
            pl.BlockSpec((2, hpb, LRU_HEAD_BLOCK, LRU_HEAD_BLOCK), lambda b, c: (0, c, 0, 0)),
            pl.BlockSpec((2, hpb, LRU_HEAD_BLOCK, LRU_HEAD_BLOCK), lambda b, c: (0, c, 0, 0)),
            pl.BlockSpec((2, lane_w), lambda b, c: (0, c)),
            pl.BlockSpec((2, lane_w), lambda b, c: (0, c)),
            pl.BlockSpec((2, lane_w), lambda b, c: (0, c)),
            pl.BlockSpec((None, 2, lane_w), lambda b, c: (b, 0, c)),
        ] + extra_specs,
        out_specs=(pl.BlockSpec((seq_len, lane_w), lambda b, c: (rb0 + b, c)),
                   pl.BlockSpec((None, 2, lane_w), lambda b, c: (b, 0, c))),
        scratch_shapes=[pltpu.VMEM((seq_len, lane_w), F32)] * 4,
        compiler_params=_cparams("arbitrary", "arbitrary"),
        name="rglru",
    )(proj, proj, conv_w, conv_b, wa_bd, wx_bd, b_a, b_x, sp, h0, *extra_args)


def _block_diag_heads(w):
    two, h, hd, _ = w.shape
    per = LRU_HEAD_BLOCK // hd
    wb = w.reshape(two, h // per, per, hd, hd)
    eye = jnp.eye(per, dtype=w.dtype)
    bd = jnp.einsum('dbkij,kl->dbkilj', wb, eye)
    return bd.reshape(two, h // per, LRU_HEAD_BLOCK, LRU_HEAD_BLOCK).astype(BF16)


def _softmax_pv(parts):
    m = None
    for s, _ in parts:
        mm = jnp.max(s, axis=-1, keepdims=True)
        m = mm if m is None else jnp.maximum(m, mm)
    acc, den = None, None
    for s, v in parts:
        p = jnp.exp(s - m)
        l = jnp.sum(p, axis=-1, keepdims=True)
        o = jnp.dot(p.astype(BF16), v, preferred_element_type=F32)
        acc = o if acc is None else acc + o
        den = l if den is None else den + l
    return acc / den


def _qk(q, k):
    return lax.dot_general(q, k, (((1,), (1,)), ((), ())), preferred_element_type=F32)


def _ctx_attn_kernel(q_ref, k_ref, v_ref, *rest, n_heads):
    o_ref, ck_ref, cv_ref = rest[-3:]
    dh = q_ref.shape[1] // n_heads
    scale = dh ** -0.5
    ck_ref[...] = k_ref[...]
    cv_ref[...] = v_ref[...]
    for h in range(n_heads):
        lanes = slice(h * dh, (h + 1) * dh)
        q = q_ref[:, lanes].astype(BF16)
        k = k_ref[:, lanes].astype(BF16)
        v = v_ref[:, lanes].astype(BF16)
        o_ref[:, lanes] = _softmax_pv([(_qk(q, k) * scale, v)]).astype(o_ref.dtype)


def _ctx_attention(qkv, n_seq, seq_len, n_heads, total_rows, layer_j, n_attn_layers, caches):
    d = qkv.shape[1] // 3
    cache_shape = jax.ShapeDtypeStruct((n_seq, n_attn_layers, seq_len, d), F32)
    cache_spec = pl.BlockSpec((None, None, seq_len, d), lambda b: (b, layer_j, 0, 0))
    in_specs = [pl.BlockSpec((seq_len, d), lambda b, cb=cb: (b, cb)) for cb in range(3)]
    args = [qkv, qkv, qkv]
    aliases = {}
    if caches is not None:
        in_specs += [pl.BlockSpec(memory_space=pl.ANY)] * 2
        args += list(caches)
        aliases = {3: 1, 4: 2}
    return pl.pallas_call(
        functools.partial(_ctx_attn_kernel, n_heads=n_heads),
        out_shape=(jax.ShapeDtypeStruct((total_rows, d), BF16), cache_shape, cache_shape),
        grid=(n_seq,),
        in_specs=in_specs,
        out_specs=(pl.BlockSpec((seq_len, d), lambda b: (b, 0)), cache_spec, cache_spec),
        input_output_aliases=aliases,
        compiler_params=_cparams("arbitrary"),
        name="ctx_attention",
    )(*args)


def _nbr_row_windows(rows):
    kr = min(WIN_ROWS_MAX, rows)
    starts = np.clip(np.arange(rows) - kr // 2, 0, rows - kr)
    groups, r = [], 0
    while r < rows:
        r1 = r
        while r1 < rows and starts[r1] == starts[r]:
            r1 += 1
        groups.append((r, r1, int(starts[r])))
        r = r1
    return kr, starts, groups


def _nbr_attn_kernel(q_ref, k_ref, v_ref, kc_ref, vc_ref, bias_ref, o_in_ref, o_ref):
    del o_in_ref
    dh = q_ref.shape[1]
    scale = dh ** -0.5
    kr, starts, groups = _nbr_row_windows(q_ref.shape[0] // GRID_W)
    k_all = k_ref[...].astype(BF16)
    v_all = v_ref[...].astype(BF16)
    kc = kc_ref[...].astype(BF16)
    vc = vc_ref[...].astype(BF16)
    for r0, r1, rs in groups:
        q_rows = slice(r0 * GRID_W, r1 * GRID_W)
        k_rows = slice(rs * GRID_W, (rs + kr) * GRID_W)
        q = q_ref[q_rows, :].astype(BF16)
        bias = jnp.concatenate([bias_ref[int(starts[r]) - r + WIN_ROWS_MAX - 1] for r in range(r0, r1)], axis=0)
        s_loc = _qk(q, k_all[k_rows]) * scale + bias
        s_ctx = _qk(q, kc) * scale
        o_ref[q_rows, :] = _softmax_pv([(s_loc, v_all[k_rows]), (s_ctx, vc)]).astype(o_ref.dtype)


def _nbr_attention(qkv, o_all, cache_k, cache_v, layer_j, bias, row0, n_seq, seq_len, n_heads):
    d = qkv.shape[1] // 3
    dh = d // n_heads
    rb0 = row0 // seq_len
    past = cache_k.shape[2]
    ck = cache_k.reshape(cache_k.shape[0], cache_k.shape[1], past, d)
    cv = cache_v.reshape(ck.shape)
    cache_spec = pl.BlockSpec((None, None, past, dh), lambda h, b: (b, layer_j, 0, h))
    return pl.pallas_call(
        _nbr_attn_kernel,
        out_shape=jax.ShapeDtypeStruct(o_all.shape, o_all.dtype),
        grid=(n_heads, n_seq),
        in_specs=[
            pl.BlockSpec((seq_len, dh), lambda h, b: (rb0 + b, h)),
            pl.BlockSpec((seq_len, dh), lambda h, b: (rb0 + b, n_heads + h)),
            pl.BlockSpec((seq_len, dh), lambda h, b: (rb0 + b, 2 * n_heads + h)),
            cache_spec, cache_spec,
            pl.BlockSpec((None,) + bias.shape[1:], lambda h, b: (h, 0, 0, 0)),
            pl.BlockSpec(memory_space=pl.ANY),
        ],
        out_specs=pl.BlockSpec((seq_len, dh), lambda h, b: (rb0 + b, h)),
        input_output_aliases={6: 0},
        compiler_params=_cparams("arbitrary", "arbitrary"),
        name="nbr_attention",
    )(qkv, qkv, qkv, ck, cv, bias, o_all)


def _nbr_bias(rpb, rows):
    kr, _, _ = _nbr_row_windows(rows)
    c_idx = np.arange(GRID_W)
    col_start = np.clip(c_idx - WIN_COLS // 2, 0, GRID_W - WIN_COLS)
    kcol = np.arange(GRID_W)[None, :]
    col_valid = (kcol >= col_start[:, None]) & (kcol < col_start[:, None] + WIN_COLS)
    col_off = np.clip(kcol - c_idx[:, None] + WIN_COLS - 1, 0, 2 * WIN_COLS - 2)
    col_sel = jnp.asarray(np.eye(2 * WIN_COLS - 1, dtype=np.float32)[col_off])
    tab = jnp.einsum('hij,cmj->hcim', rpb.astype(F32), col_sel, precision=lax.Precision.HIGHEST)
    tab = jnp.where(jnp.asarray(col_valid)[None, :, None, :], tab, NEG_INF)
    n_win = 2 * WIN_ROWS_MAX - kr
    wins = [tab[:, :, i0:i0 + kr, :].reshape(tab.shape[0], GRID_W, kr * GRID_W) for i0 in range(n_win)]
    return jnp.stack(wins, axis=1)


def _slab_copy(hbm, token, buf, slot, r, nk, sem):
    return pltpu.make_async_copy(hbm.at[pl.ds(pl.multiple_of(token * nk, nk), nk), :],
                                 buf.at[slot, pl.ds(pl.multiple_of(r * nk, nk), nk), :], sem.at[slot])


def _moe_gather_kernel(src_ref, nv_ref, h_hbm, o_ref, buf, sem):
    i = pl.program_id(0)
    tm, d = o_ref.shape
    nk = d // LANES
    slot = i % 2

    def issue_tile(tile, s):
        base = tile * tm

        def issue(r, c):
            _slab_copy(h_hbm, src_ref[base + r], buf, s, r, nk, sem).start()
            return c

        lax.fori_loop(0, tm, issue, 0, unroll=8)

    @pl.when(i == 0)
    def _():
        issue_tile(0, 0)

    @pl.when(i + 1 < nv_ref[0])
    def _():
        issue_tile(i + 1, 1 - slot)

    @pl.when(i < nv_ref[0])
    def _():
        pltpu.make_async_copy(h_hbm.at[pl.ds(0, tm * nk), :], buf.at[slot], sem.at[slot]).wait()
        for k in range(nk):
            o_ref[:, k * LANES:(k + 1) * LANES] = buf[slot, pl.ds(k, tm, stride=nk), :].astype(o_ref.dtype)

    @pl.when(i >= nv_ref[0])
    def _():
        o_ref[...] = jnp.zeros_like(o_ref)


def _moe_gather(h_slabs, d, src, n_valid, n_rows):
    tm = MOE_TM
    nk = d // LANES
    return pl.pallas_call(
        _moe_gather_kernel,
        out_shape=jax.ShapeDtypeStruct((n_rows, d), BF16),
        grid_spec=pltpu.PrefetchScalarGridSpec(
            num_scalar_prefetch=2,
            grid=(n_rows // tm,),
            in_specs=[pl.BlockSpec(memory_space=pl.ANY)],
            out_specs=pl.BlockSpec((tm, d), lambda i, src, nv: (i, 0)),
            scratch_shapes=[pltpu.VMEM((2, tm * nk, LANES), F32), pltpu.SemaphoreType.DMA((2,))],
        ),
        compiler_params=_cparams("arbitrary"),
        name="moe_gather",
    )(src, n_valid, h_slabs)


def _new_expert(te_ref, i):
    return (i == 0) | (te_ref[i] != te_ref[jnp.maximum(i - 1, 0)])


def _moe_gu_kernel(te_ref, nv_ref, x_ref, wg_ref, wu_ref, o_ref, wgb, wub):
    i = pl.program_id(1)
    valid = i < nv_ref[0]

    @pl.when(valid & _new_expert(te_ref, i))
    def _():
        wgb[...] = wg_ref[...].astype(BF16)
        wub[...] = wu_ref[...].astype(BF16)

    @pl.when(valid)
    def _():
        x = x_ref[...]
        a = jnp.dot(x, wgb[...], preferred_element_type=F32)
        b = jnp.dot(x, wub[...], preferred_element_type=F32)
        o_ref[...] = (a * jax.nn.sigmoid(a) * b).astype(o_ref.dtype)

    @pl.when(jnp.logical_not(valid))
    def _():
        o_ref[...] = jnp.zeros_like(o_ref)


def _moe_gate_up(xs, w_gu, layer, tile_expert, n_valid):
    p, d = xs.shape
    de = w_gu.shape[3] // 2
    tm, tn = MOE_TM, min(MOE_TN, de)
    nj = de // tn
    row = lambda i, nv: jnp.minimum(i, nv[0] - 1)
    return pl.pallas_call(
        _moe_gu_kernel,
        out_shape=jax.ShapeDtypeStruct((p, de), BF16),
        grid_spec=pltpu.PrefetchScalarGridSpec(
            num_scalar_prefetch=2,
            grid=(nj, p // tm),
            in_specs=[
                pl.BlockSpec((tm, d), lambda j, i, te, nv: (row(i, nv), 0)),
                pl.BlockSpec((None, None, d, tn), lambda j, i, te, nv: (layer, te[i], 0, j)),
                pl.BlockSpec((None, None, d, tn), lambda j, i, te, nv: (layer, te[i], 0, nj + j)),
            ],
            out_specs=pl.BlockSpec((tm, tn), lambda j, i, te, nv: (i, j)),
            scratch_shapes=[pltpu.VMEM((d, tn), BF16), pltpu.VMEM((d, tn), BF16)],
        ),
        compiler_params=_cparams("arbitrary", "arbitrary"),
        name="moe_gate_up",
    )(tile_expert, n_valid, xs, w_gu, w_gu)


def _moe_dn_kernel(te_ref, nv_ref, h_ref, w_ref, o_ref, wb):
    i = pl.program_id(0)
    valid = i < nv_ref[0]

    @pl.when(valid & _new_expert(te_ref, i))
    def _():
        wb[...] = w_ref[...].astype(BF16)

    @pl.when(valid)
    def _():
        _store_slabs(o_ref, jnp.dot(h_ref[...], wb[...], preferred_element_type=F32))

    @pl.when(jnp.logical_not(valid))
    def _():
        o_ref[...] = jnp.zeros_like(o_ref)


def _moe_down(hmid, w_dn, layer, tile_expert, n_valid):
    p, de = hmid.shape
    d = w_dn.shape[3]
    nk = d // LANES
    tm = MOE_TM
    row = lambda i, nv: jnp.minimum(i, nv[0] - 1)
    return pl.pallas_call(
        _moe_dn_kernel,
        out_shape=jax.ShapeDtypeStruct((p * nk, LANES), F32),
        grid_spec=pltpu.PrefetchScalarGridSpec(
            num_scalar_prefetch=2,
            grid=(p // tm,),
            in_specs=[
                pl.BlockSpec((tm, de), lambda i, te, nv: (row(i, nv), 0)),
                pl.BlockSpec((None, None, de, d), lambda i, te, nv: (layer, te[i], 0, 0)),
            ],
            out_specs=pl.BlockSpec((tm * nk, LANES), lambda i, te, nv: (i, 0)),
            scratch_shapes=[pltpu.VMEM((de, d), BF16)],
        ),
        compiler_params=_cparams("arbitrary"),
        name="moe_down",
    )(tile_expert, n_valid, hmid, w_dn)


def _moe_combine_kernel(p1_ref, p2_ref, y_hbm, x_ref, w_ref, g_ref, o_ref, buf1, buf2, sem):
    i = pl.program_id(0)
    tm, d = o_ref.shape
    nk = d // LANES
    slot = i % 2

    def issue_tile(tile, s):
        base = tile * tm

        def issue(r, c):
            _slab_copy(y_hbm, p1_ref[base + r], buf1, s, r, nk, sem).start()
            _slab_copy(y_hbm, p2_ref[base + r], buf2, s, r, nk, sem).start()
            return c

        lax.fori_loop(0, tm, issue, 0, unroll=8)

    @pl.when(i == 0)
    def _():
        issue_tile(0, 0)

    @pl.when(i + 1 < pl.num_programs(0))
    def _():
        issue_tile(i + 1, 1 - slot)

    pltpu.make_async_copy(y_hbm.at[pl.ds(0, tm * nk), :], buf1.at[slot], sem.at[slot]).wait()
    pltpu.make_async_copy(y_hbm.at[pl.ds(0, tm * nk), :], buf2.at[slot], sem.at[slot]).wait()
    w = w_ref[...]
    w1, w2 = w[:, 0:1], w[:, 1:2]
    for k in range(nk):
        lanes = slice(k * LANES, (k + 1) * LANES)
        rows = pl.ds(k, tm, stride=nk)
        y = w1 * buf1[slot, rows, :] + w2 * buf2[slot, rows, :]
        o_ref[:, lanes] = x_ref[:, lanes] + g_ref[:, lanes] * y


def _moe_combine(y_sorted, x, wts, p1, p2, mods3, layer, lay):
    t, d = x.shape
    tm = ROW_TILE
    return pl.pallas_call(
        _moe_combine_kernel,
        out_shape=jax.ShapeDtypeStruct((t, d), F32),
        grid_spec=pltpu.PrefetchScalarGridSpec(
            num_scalar_prefetch=2,
            grid=(t // tm,),
            in_specs=[
                pl.BlockSpec(memory_space=pl.ANY),
                pl.BlockSpec((tm, d), lambda i, p1, p2: (i, 0)),
                pl.BlockSpec((tm, wts.shape[1]), lambda i, p1, p2: (i, 0)),
                pl.BlockSpec((None, 1, d), lambda i, p1, p2: (_mod_index(layer, 5, lay.mod_row(i, tm)), 0, 0)),
            ],
            out_specs=pl.BlockSpec((tm, d), lambda i, p1, p2: (i, 0)),
            scratch_shapes=[pltpu.VMEM((2, tm * (d // LANES), LANES), F32),
                            pltpu.VMEM((2, tm * (d // LANES), LANES), F32), pltpu.SemaphoreType.DMA((2,))],
        ),
        compiler_params=_cparams("arbitrary"),
        name="moe_combine",
    )(p1, p2, y_sorted, x, wts, mods3)


def _moe_plan(route, n_experts):
    t = route.shape[1]
    tm = MOE_TM
    eid = route[0:2].astype(jnp.int32).reshape(-1)
    onehot = (eid[:, None] == jnp.arange(n_experts, dtype=jnp.int32)[None, :]).astype(jnp.int32)
    csum = jnp.cumsum(onehot, axis=0)
    rank = jnp.take_along_axis(csum, eid[:, None], axis=1)[:, 0] - 1
    counts = csum[-1]
    padded = ((counts + tm - 1) // tm) * tm
    ends = jnp.cumsum(padded)
    offs = ends - padded
    pos = offs[eid] + rank
    n_rows = 2 * t + n_experts * tm
    tok = jnp.tile(jnp.arange(t, dtype=jnp.int32), 2)
    src = jnp.zeros((n_rows,), jnp.int32).at[pos].set(tok)
    n_tiles = n_rows // tm
    n_valid = (ends[-1] // tm).astype(jnp.int32)
    starts = jnp.arange(n_tiles, dtype=jnp.int32) * tm
    te = jnp.sum((starts[:, None] >= ends[None, :]).astype(jnp.int32), axis=1)
    te_last = jnp.sum((((n_valid - 1) * tm) >= ends).astype(jnp.int32))
    te = jnp.where(starts < ends[-1], te, te_last).astype(jnp.int32)
    wts = jnp.transpose(route[2:4])
    wts = jnp.pad(wts, ((0, 0), (0, 6)))
    return src, te, n_valid.reshape(1), pos[:t], pos[t:], wts, n_rows


def _moe_layer(x, h2_slabs, route, mods3, layer, lay, w_gu, w_dn):
    n_experts = w_gu.shape[1]
    src, te, n_valid, p1, p2, wts, n_rows = _moe_plan(route, n_experts)
    xs = _moe_gather(h2_slabs, x.shape[1], src, n_valid, n_rows)
    hmid = _moe_gate_up(xs, w_gu, layer, te, n_valid)
    ys = _moe_down(hmid, w_dn, layer, te, n_valid)
    return _moe_combine(ys, x, wts, p1, p2, mods3, layer, lay)


def kernel(x_prompt, x_sample, state_s5_re, state_s5_im, state_lru, cache_attn_k, cache_attn_v, c, c_ctx, w_ada, b_ada, norm1_g, norm2_g, final_norm_g, w_in_even, w_out_even, s5_lam_re, s5_lam_im, s5_log_dt, s5_b_re, s5_b_im, s5_c_re, s5_c_im, s5_d, s5_w_glu, s5_b_glu, lru_conv_w, lru_conv_b, lru_w_a, lru_b_a, lru_w_x, lru_b_x, lru_lam, w_qkv, w_o, rpb, w_router, b_router, w_gate_up, w_down):
    batch, seq, d = x_prompt.shape
    dec_batch, dec_seq, _ = x_sample.shape
    depth = w_ada.shape[0]
    n_heads = cache_attn_k.shape[3]
    s5_w = s5_d.shape[1]
    lru_w = lru_conv_w.shape[2]
    n_groups, n_state = s5_lam_re.shape[2], s5_lam_re.shape[3]
    assert dec_batch < MOD_ROWS
    n_ctx = batch * seq
    assert n_ctx % MM_TM == 0 and dec_seq % MM_TM == 0 and seq % ROW_TILE == 0 and dec_seq % ROW_TILE == 0
    lay = _Layout(n_ctx, dec_batch, dec_seq)
    t = lay.total

    x = jnp.concatenate([x_prompt.reshape(n_ctx, d), x_sample.reshape(dec_batch * dec_seq, d)], axis=0)
    cvec = jnp.zeros((MOD_ROWS, d), F32).at[:dec_batch].set(c).at[dec_batch].set(c_ctx)
    mods = _ada_project(cvec, w_ada, b_ada)
    mods3 = mods.reshape(depth * MOD_ROWS * N_MOD, 1, d)
    w_router_t = jnp.transpose(w_router)
    b_router_col = b_router.reshape(-1, 1)

    s5_re_list, s5_im_list, lru_list = [], [], []
    kv_caches = None
    for l in range(depth):
        j = l // 2
        h1 = _norm_mod(x, norm1_g[l].reshape(1, d), mods3, l, 0, lay, BF16)
        if l % 2 == 0:
            proj = _matmul([h1], w_in_even, j, F32)
            mats = _s5_matrices(s5_lam_re[j], s5_lam_im[j], s5_log_dt[j], s5_b_re[j], s5_b_im[j],
                                s5_c_re[j], s5_c_im[j])
            zero_h0 = jnp.zeros((batch, 2, n_groups, 2 * n_state), F32)
            lat_h0 = jnp.concatenate([state_s5_re[:, j], state_s5_im[:, j]], axis=-1).astype(F32)
            y_scan, s5_fin = _s5_scan(proj, 0, zero_h0, mats, batch, seq, s5_w, None)
            y_scan, _ = _s5_scan(proj, n_ctx, lat_h0, mats, dec_batch, dec_seq, s5_w, y_scan)
            s5_re_list.append(s5_fin[..., :n_state])
            s5_im_list.append(s5_fin[..., n_state:])
            y_s5 = _s5_glu(y_scan, proj, s5_d[j].reshape(1, s5_w), s5_w_glu, j, s5_b_glu[j].reshape(1, s5_w))
            wa_bd = _block_diag_heads(lru_w_a[j])
            wx_bd = _block_diag_heads(lru_w_x[j])
            sp = jax.nn.softplus(-lru_lam[j].astype(F32))
            lru_args = (lru_conv_w[j], lru_conv_b[j].reshape(1, lru_w), wa_bd, wx_bd, lru_b_a[j], lru_b_x[j], sp)
            y_lru, lru_fin = _lru_mixer(proj, 0, batch, seq, lru_w, jnp.zeros((batch, 2, lru_w), F32),
                                        *lru_args, None)
            y_lru, _ = _lru_mixer(proj, n_ctx, dec_batch, dec_seq, lru_w // 2, state_lru[:, j].astype(F32),
                                  *lru_args, y_lru)
            lru_list.append(lru_fin)
            x = _matmul([y_s5, y_lru], w_out_even, j, F32, resid=(x, mods3, l, 2, lay))
        else:
            qkv = _matmul([h1], w_qkv, j, F32)
            o_all, *kv_caches = _ctx_attention(qkv, batch, seq, n_heads, t, j, depth // 2, kv_caches)
            bias = _nbr_bias(rpb[j], dec_seq // GRID_W)
            o_all = _nbr_attention(qkv, o_all, cache_attn_k, cache_attn_v, j, bias, n_ctx, dec_batch, dec_seq, n_heads)
            x = _matmul([o_all], w_o, j, F32, resid=(x, mods3, l, 2, lay))
        h2, route = _norm_mod_route(x, norm2_g[l].reshape(1, d), mods3, l, lay, w_router_t, b_router_col)
        x = _moe_layer(x, h2, route, mods3, l, lay, w_gate_up, w_down)

    g_fin = final_norm_g.reshape(1, d)
    y_prompt = _final_norm(x, g_fin, 0, n_ctx).reshape(batch, seq, d)
    y_sample = _final_norm(x, g_fin, n_ctx, t - n_ctx).reshape(dec_batch, dec_seq, d)
    cache_shape = (batch, depth // 2, seq, n_heads, d // n_heads)
    return (y_prompt, y_sample, jnp.stack(s5_re_list, axis=1), jnp.stack(s5_im_list, axis=1),
            jnp.stack(lru_list, axis=1), kv_caches[0].reshape(cache_shape), kv_caches[1].reshape(cache_shape))
```

```python
import functools
import math

import numpy as np
import jax
import jax.numpy as jnp
from jax import lax
from jax.experimental import pallas as pl
from jax.experimental.pallas import tpu as pltpu

F32 = jnp.float32
BF16 = jnp.bfloat16

LANES = 128
N_DMA_PRIORITIES = 2
NORM_EPS = 1e-6
NEG_INF = -1e30
S5_GROUP = 16
S5_CHUNK = 16
S5_LANE_TILE = 256
S5_ROW_BLOCK = 2048
S5_TRANSPOSE_ROWS = 64
LRU_C = 8.0
LRU_HEAD_BLOCK = 256
N_EXPERT_GROUPS = 4
WIN_ROWS_MAX = 8
WIN_COLS = 16
GRID_W = 64
N_MOD = 6
MOD_ROWS = 8
VMEM_LIMIT = 52 * 1024 * 1024
ROW_TILE = 256
MM_TM = 1024
MM_TN = 1024
MM_TN_RESID = 512
MOE_TM = 512
MOE_TN = 512


def _cparams(*sem):
    return pltpu.CompilerParams(dimension_semantics=sem, vmem_limit_bytes=VMEM_LIMIT)


def _gelu(x):
    return 0.5 * x * (1.0 + jnp.tanh(math.sqrt(2.0 / math.pi) * (x + 0.044715 * (x * x * x))))


class _Layout:
    def __init__(self, n_ctx_rows, dec_batch, dec_seq):
        self.n_ctx = n_ctx_rows
        self.dec_batch = dec_batch
        self.dec_seq = dec_seq
        self.total = n_ctx_rows + dec_batch * dec_seq

    def mod_row(self, i, tm):
        nct = self.n_ctx // tm
        per = self.dec_seq // tm
        return jnp.where(i < nct, self.dec_batch, (i - nct) // per)


def _mod_index(layer, which, row):
    return (layer * MOD_ROWS + row) * N_MOD + which


def _ada_kernel(c_ref, w_ref, b_ref, o_ref):
    c = c_ref[...]
    s = (c * jax.nn.sigmoid(c)).astype(BF16)
    o_ref[...] = jnp.dot(s, w_ref[...].astype(BF16), preferred_element_type=F32) + b_ref[...]


def _ada_project(cvec, w_ada, b_ada):
    depth, d, n = w_ada.shape
    tn = MM_TN
    return pl.pallas_call(
        _ada_kernel,
        out_shape=jax.ShapeDtypeStruct((depth, MOD_ROWS, n), F32),
        grid=(depth, n // tn),
        in_specs=[
            pl.BlockSpec((MOD_ROWS, d), lambda l, j: (0, 0)),
            pl.BlockSpec((None, d, tn), lambda l, j: (l, 0, j)),
            pl.BlockSpec((None, 1, tn), lambda l, j: (l, 0, j)),
        ],
        out_specs=pl.BlockSpec((None, MOD_ROWS, tn), lambda l, j: (l, 0, j)),
        compiler_params=_cparams("arbitrary", "arbitrary"),
        name="ada_project",
    )(cvec, w_ada, b_ada.reshape(depth, 1, n))


def _rms(x, g):
    ms = jnp.mean(x * x, axis=-1, keepdims=True)
    return x * lax.rsqrt(ms + NORM_EPS) * g


def _norm_mod_kernel(x_ref, g_ref, sh_ref, sc_ref, o_ref):
    y = _rms(x_ref[...], g_ref[...])
    o_ref[...] = (y * (1.0 + sc_ref[...]) + sh_ref[...]).astype(o_ref.dtype)


def _norm_mod(x, g_row, mods3, layer, which_shift, lay, out_dtype):
    t, d = x.shape
    tm = ROW_TILE
    row = lambda i: lay.mod_row(i, tm)
    return pl.pallas_call(
        _norm_mod_kernel,
        out_shape=jax.ShapeDtypeStruct((t, d), out_dtype),
        grid=(t // tm,),
        in_specs=[
            pl.BlockSpec((tm, d), lambda i: (i, 0)),
            pl.BlockSpec((1, d), lambda i: (0, 0)),
            pl.BlockSpec((None, 1, d), lambda i: (_mod_index(layer, which_shift, row(i)), 0, 0)),
            pl.BlockSpec((None, 1, d), lambda i: (_mod_index(layer, which_shift + 1, row(i)), 0, 0)),
        ],
        out_specs=pl.BlockSpec((tm, d), lambda i: (i, 0)),
        compiler_params=_cparams("arbitrary"),
        name="norm_mod",
    )(x, g_row, mods3, mods3)


def _top2_of4(a):
    m1 = jnp.maximum(jnp.maximum(a[0], a[1]), jnp.maximum(a[2], a[3]))
    i1 = jnp.where(a[0] == m1, 0, jnp.where(a[1] == m1, 1, jnp.where(a[2] == m1, 2, 3)))
    b = [jnp.where(i1 == k, -jnp.inf, a[k]) for k in range(4)]
    m2 = jnp.maximum(jnp.maximum(b[0], b[1]), jnp.maximum(b[2], b[3]))
    i2 = jnp.where(b[0] == m2, 0, jnp.where(b[1] == m2, 1, jnp.where(b[2] == m2, 2, 3)))
    return m1 + m2, i1, i2


def _norm_mod_route_kernel(x_ref, g_ref, sh_ref, sc_ref, wr_ref, br_ref, o_ref, r_ref):
    y = _rms(x_ref[...], g_ref[...])
    h = y * (1.0 + sc_ref[...]) + sh_ref[...]
    o_ref[...] = h
    logits = lax.dot_general(wr_ref[...], h, (((1,), (1,)), ((), ())),
                             precision=lax.Precision.HIGHEST, preferred_element_type=F32)
    scores = jax.nn.sigmoid(logits)
    sel = scores + br_ref[...]
    n_e = scores.shape[0]
    per = n_e // N_EXPERT_GROUPS
    sel_rows = [sel[e:e + 1, :] for e in range(n_e)]
    score_rows = [scores[e:e + 1, :] for e in range(n_e)]
    gs, i1s, i2s = [], [], []
    for gi in range(N_EXPERT_GROUPS):
        s, i1, i2 = _top2_of4(sel_rows[gi * per:(gi + 1) * per])
        gs.append(s)
        i1s.append(i1)
        i2s.append(i2)
    gmax = jnp.maximum(jnp.maximum(gs[0], gs[1]), jnp.maximum(gs[2], gs[3]))
    gsel = jnp.where(gs[0] == gmax, 0, jnp.where(gs[1] == gmax, 1, jnp.where(gs[2] == gmax, 2, 3)))
    l1 = jnp.where(gsel == 0, i1s[0], jnp.where(gsel == 1, i1s[1], jnp.where(gsel == 2, i1s[2], i1s[3])))
    l2 = jnp.where(gsel == 0, i2s[0], jnp.where(gsel == 1, i2s[1], jnp.where(gsel == 2, i2s[2], i2s[3])))
    e1 = gsel * per + l1
    e2 = gsel * per + l2
    w1 = jnp.zeros_like(gmax)
    w2 = jnp.zeros_like(gmax)
    for e in range(n_e):
        w1 = jnp.where(e1 == e, score_rows[e], w1)
        w2 = jnp.where(e2 == e, score_rows[e], w2)
    wsum = w1 + w2
    zero = jnp.zeros_like(gmax)
    r_ref[...] = jnp.concatenate(
        [e1.astype(F32), e2.astype(F32), w1 / wsum, w2 / wsum, zero, zero, zero, zero], axis=0)


def _norm_mod_route(x, g_row, mods3, layer, lay, w_router_t, b_router_col):
    t, d = x.shape
    tm = ROW_TILE
    n_e = w_router_t.shape[0]
    row = lambda i: lay.mod_row(i, tm)
    return pl.pallas_call(
        _norm_mod_route_kernel,
        out_shape=(jax.ShapeDtypeStruct((t, d), F32), jax.ShapeDtypeStruct((8, t), F32)),
        grid=(t // tm,),
        in_specs=[
            pl.BlockSpec((tm, d), lambda i: (i, 0)),
            pl.BlockSpec((1, d), lambda i: (0, 0)),
            pl.BlockSpec((None, 1, d), lambda i: (_mod_index(layer, 3, row(i)), 0, 0)),
            pl.BlockSpec((None, 1, d), lambda i: (_mod_index(layer, 4, row(i)), 0, 0)),
            pl.BlockSpec((n_e, d), lambda i: (0, 0)),
            pl.BlockSpec((n_e, 1), lambda i: (0, 0)),
        ],
        out_specs=(pl.BlockSpec((tm, d), lambda i: (i, 0)), pl.BlockSpec((8, tm), lambda i: (0, i))),
        compiler_params=_cparams("arbitrary"),
        name="norm_mod_route",
    )(x, g_row, mods3, mods3, w_router_t, b_router_col)


def _final_norm_kernel(x_ref, g_ref, o_ref):
    o_ref[...] = _rms(x_ref[...], g_ref[...])


def _final_norm(x, g_row, row0, n_rows):
    d = x.shape[1]
    tm = ROW_TILE
    return pl.pallas_call(
        _final_norm_kernel,
        out_shape=jax.ShapeDtypeStruct((n_rows, d), F32),
        grid=(n_rows // tm,),
        in_specs=[pl.BlockSpec((tm, d), lambda i: (row0 // tm + i, 0)), pl.BlockSpec((1, d), lambda i: (0, 0))],
        out_specs=pl.BlockSpec((tm, d), lambda i: (i, 0)),
        compiler_params=_cparams("arbitrary"),
        name="final_norm",
    )(x, g_row)


def _mm_kernel(*refs, n_a, resid):
    a_refs = refs[:n_a]
    w_ref = refs[n_a]
    pos = n_a + 1
    if resid:
        x_ref, g_ref = refs[pos], refs[pos + 1]
        pos += 2
    o_ref, wb_ref = refs[pos], refs[pos + 1]

    @pl.when(pl.program_id(1) == 0)
    def _():
        wb_ref[...] = w_ref[...].astype(BF16)

    acc = None
    k0 = 0
    for a_ref in a_refs:
        ka = a_ref.shape[1]
        part = jnp.dot(a_ref[...], wb_ref[k0:k0 + ka, :], preferred_element_type=F32)
        acc = part if acc is None else acc + part
        k0 += ka
    if resid:
        o_ref[...] = x_ref[...] + g_ref[...] * acc
    else:
        o_ref[...] = acc.astype(o_ref.dtype)


def _matmul(a_list, w, w_layer, out_dtype, resid=None):
    m = a_list[0].shape[0]
    _, k, n = w.shape
    tm = MM_TM
    tn = min(MM_TN_RESID if resid is not None else MM_TN, n)
    in_specs = [pl.BlockSpec((tm, a.shape[1]), lambda j, i: (i, 0)) for a in a_list]
    in_specs.append(pl.BlockSpec((None, k, tn), lambda j, i: (w_layer, 0, j)))
    args = list(a_list) + [w]
    if resid is not None:
        x, mods3, layer, which, lay = resid
        in_specs.append(pl.BlockSpec((tm, tn), lambda j, i: (i, j)))
        in_specs.append(pl.BlockSpec(
            (None, 1, tn), lambda j, i: (_mod_index(layer, which, lay.mod_row(i, tm)), 0, j)))
        args += [x, mods3]
    return pl.pallas_call(
        functools.partial(_mm_kernel, n_a=len(a_list), resid=resid is not None),
        out_shape=jax.ShapeDtypeStruct((m, n), out_dtype),
        grid=(n // tn, m // tm),
        in_specs=in_specs,
        out_specs=pl.BlockSpec((tm, tn), lambda j, i: (i, j)),
        scratch_shapes=[pltpu.VMEM((k, tn), BF16)],
        compiler_params=_cparams("arbitrary", "arbitrary"),
        name="matmul_resid" if resid is not None else "matmul",
    )(*args)


def _cmul(ar, ai, br, bi):
    return ar * br - ai * bi, ar * bi + ai * br


def _s5_matrices(lam_re, lam_im, log_dt, b_re, b_im, c_re, c_im):
    s = S5_CHUNK
    f = lambda z: z.astype(F32)
    lam_re, lam_im, log_dt, b_re, b_im, c_re, c_im = map(f, (lam_re, lam_im, log_dt, b_re, b_im, c_re, c_im))
    dt = jnp.exp(log_dt)[..., None]
    kk = jnp.arange(s + 1, dtype=F32)[:, None, None, None]
    mag = jnp.exp(kk * (lam_re * dt)[None])
    ph = kk * (lam_im * dt)[None]
    pw_re, pw_im = mag * jnp.cos(ph), mag * jnp.sin(ph)
    a_re, a_im = pw_re[1], pw_im[1]
    den = lam_re * lam_re + lam_im * lam_im
    q_re, q_im = _cmul(a_re - 1.0, a_im, lam_re / den, -lam_im / den)
    bb_re, bb_im = _cmul(q_re[..., None], q_im[..., None], b_re, b_im)
    cp_re, cp_im = _cmul(c_re[:, :, None], c_im[:, :, None],
                         jnp.moveaxis(pw_re, 0, 2)[:, :, :, None, :], jnp.moveaxis(pw_im, 0, 2)[:, :, :, None, :])
    bt_re = jnp.swapaxes(bb_re, 2, 3)[:, :, None, None]
    bt_im = jnp.swapaxes(bb_im, 2, 3)[:, :, None, None]
    kern = jnp.sum(cp_re[:, :, :, :, None, :] * bt_re - cp_im[:, :, :, :, None, :] * bt_im, axis=-1)
    sp = np.arange(s)[:, None]
    so = np.arange(s)[None, :]
    lag_f = np.clip(so - sp, 0, s)
    lag_b = np.clip(sp - so, 0, s)
    kf = jnp.where(jnp.asarray(so >= sp)[None, :, :, None, None], kern[0][:, lag_f], 0.0)
    kb = jnp.where(jnp.asarray(sp >= so)[None, :, :, None, None], kern[1][:, lag_b], 0.0)
    g = kern.shape[1]
    n = s * S5_GROUP
    tmat = jnp.transpose(kf + kb, (0, 1, 4, 2, 3)).reshape(g, n, n)

    def e_mat(d, powers):
        pr = jnp.moveaxis(pw_re[powers, d], 0, 1)[:, :, :, None]
        pi = jnp.moveaxis(pw_im[powers, d], 0, 1)[:, :, :, None]
        er, ei = _cmul(pr, pi, bb_re[d][:, None], bb_im[d][:, None])
        er = jnp.transpose(er, (0, 1, 3, 2)).reshape(g, n, -1)
        ei = jnp.transpose(ei, (0, 1, 3, 2)).reshape(g, n, -1)
        return jnp.concatenate([er, ei], axis=-1)

    def c_mat(d, powers):
        cr = jnp.transpose(cp_re[d][:, powers], (0, 3, 1, 2)).reshape(g, -1, n)
        ci = jnp.transpose(cp_im[d][:, powers], (0, 3, 1, 2)).reshape(g, -1, n)
        return jnp.concatenate([cr, -ci], axis=1)

    swap = lambda e: jnp.concatenate([e[..., e.shape[-1] // 2:], e[..., :e.shape[-1] // 2]], axis=-1)
    e_f = e_mat(0, np.arange(s - 1, -1, -1))
    e_b = e_mat(1, np.arange(s))
    e_all = jnp.concatenate([e_f, swap(e_f), e_b, swap(e_b)], axis=-1)
    c_all = jnp.concatenate([c_mat(0, np.arange(1, s + 1)), c_mat(1, np.arange(s, 0, -1))], axis=1)
    dec_r = jnp.concatenate([pw_re[s], pw_re[s]], axis=-1)
    dec_i = jnp.concatenate([-pw_im[s], pw_im[s]], axis=-1)
    decay = jnp.stack([dec_r, dec_i], axis=2)[:, :, :, None, :]
    return tmat.astype(BF16), e_all.astype(BF16), c_all.astype(BF16), decay


def _block_transpose(v):
    n, w = v.shape
    rows = lax.broadcasted_iota(jnp.int32, v.shape, 0)
    lanes = lax.broadcasted_iota(jnp.int32, v.shape, 1)
    k = S5_CHUNK // 2
    while k >= 1:
        up = pltpu.roll(pltpu.roll(v, n - k, 0), S5_GROUP * k, 1)
        dn = pltpu.roll(pltpu.roll(v, k, 0), w - S5_GROUP * k, 1)
        rbit = (rows & k) != 0
        gbit = (lanes & (S5_GROUP * k)) != 0
        v = jnp.where(jnp.logical_and(jnp.logical_not(rbit), gbit), up,
                      jnp.where(jnp.logical_and(rbit, jnp.logical_not(gbit)), dn, v))
        k //= 2
    return v


def _s5_kernel(x_ref, t_ref, e_ref, c_ref, dec_ref, h0_ref, *rest, nbk, nc):
    y_ref, hfin_ref, v0, v1, w0, w1, zf, zfs, zb, zbs, hsf, hsb = rest[-12:]
    lw = x_ref.shape[1]
    gpt = lw // S5_GROUP
    nr = nbk * nc
    p2 = hsf.shape[1]
    n_blk = x_ref.shape[0] // S5_TRANSPOSE_ROWS

    def to_groups(i, carry):
        rows = pl.ds(pl.multiple_of(i * S5_TRANSPOSE_ROWS, S5_TRANSPOSE_ROWS), S5_TRANSPOSE_ROWS)
        v = _block_transpose(x_ref[rows, :])
        v0[rows, :] = v[:, :LANES]
        v1[rows, :] = v[:, LANES:]
        return carry

    lax.fori_loop(0, n_blk, to_groups, 0)

    def group(g, carry):
        rows_g = pl.ds(g, nr, stride=gpt)
        u = jnp.concatenate([v0[rows_g, :], v1[rows_g, :]], axis=1).astype(BF16)
        z = jnp.dot(u, e_ref[g], preferred_element_type=F32)
        zf[...] = z[:, 0:p2]
        zfs[...] = z[:, p2:2 * p2]
        zb[...] = z[:, 2 * p2:3 * p2]
        zbs[...] = z[:, 3 * p2:4 * p2]
        ar_f, ai_f, ar_b, ai_b = dec_ref[0, g, 0], dec_ref[0, g, 1], dec_ref[1, g, 0], dec_ref[1, g, 1]
        hf, hfs, hb, hbs = h0_ref[g, 0], h0_ref[g, 1], h0_ref[g, 2], h0_ref[g, 3]
        for c in range(nc):
            rf = pl.ds(c, nbk, stride=nc)
            rb = pl.ds(nc - 1 - c, nbk, stride=nc)
            hsf[rf, :] = hf
            hsb[rb, :] = hb
            hf, hfs = ar_f * hf + ai_f * hfs + zf[rf, :], ar_f * hfs - ai_f * hf + zfs[rf, :]
            hb, hbs = ar_b * hb + ai_b * hbs + zb[rb, :], ar_b * hbs - ai_b * hb + zbs[rb, :]
        hfin_ref[g, 0] = hf
        hfin_ref[g, 1] = hb
        hs = jnp.concatenate([hsf[...], hsb[...]], axis=1).astype(BF16)
        y = jnp.dot(u, t_ref[g], preferred_element_type=F32)
        y = y + jnp.dot(hs, c_ref[g], preferred_element_type=F32)
        w0[rows_g, :] = y[:, :LANES]
        w1[rows_g, :] = y[:, LANES:]
        return carry

    lax.fori_loop(0, gpt, group, 0)

    def to_tokens(i, carry):
        rows = pl.ds(pl.multiple_of(i * S5_TRANSPOSE_ROWS, S5_TRANSPOSE_ROWS), S5_TRANSPOSE_ROWS)
        y_ref[rows, :] = _block_transpose(jnp.concatenate([w0[rows, :], w1[rows, :]], axis=1))
        return carry

    lax.fori_loop(0, n_blk, to_tokens, 0)


def _s5_scan(proj, row0, h0, mats, n_seq, seq_len, width, y_prev):
    tmat, e_all, c_all, decay = mats
    n = tmat.shape[1]
    p2 = decay.shape[-1]
    lw = S5_LANE_TILE
    gpt = lw // S5_GROUP
    nc = seq_len // S5_CHUNK
    nbk = max(1, S5_ROW_BLOCK // seq_len)
    rb_rows = nbk * seq_len
    n_rb = n_seq // nbk
    nr = nbk * nc
    rb0 = row0 // rb_rows
    assert n_seq % nbk == 0 and row0 % rb_rows == 0 and p2 == LANES and lw == 2 * LANES
    g = tmat.shape[0]
    h0s = jnp.concatenate([h0[..., p2 // 2:], h0[..., :p2 // 2]], axis=-1)
    h04 = jnp.stack([h0[:, 0], h0s[:, 0], h0[:, 1], h0s[:, 1]], axis=1)
    h0p = jnp.transpose(h04.reshape(n_rb, nbk, 4, g, p2), (0, 3, 2, 1, 4))
    wspec = lambda r, c: pl.BlockSpec((gpt, r, c), lambda lt, rb: (lt, 0, 0))
    in_specs = [
        pl.BlockSpec((rb_rows, lw), lambda lt, rb: (rb0 + rb, lt)),
        wspec(n, n), wspec(n, 4 * p2), wspec(2 * p2, n),
        pl.BlockSpec((2, gpt, 2, 1, p2), lambda lt, rb: (0, lt, 0, 0, 0)),
        pl.BlockSpec((None, gpt, 4, nbk, p2), lambda lt, rb: (rb, lt, 0, 0, 0)),
    ]
    args = [proj, tmat, e_all, c_all, decay, h0p]
    aliases = {}
    if y_prev is not None:
        in_specs.append(pl.BlockSpec(memory_space=pl.ANY))
        args.append(y_prev)
        aliases = {6: 0}
    y, hfin = pl.pallas_call(
        functools.partial(_s5_kernel, nbk=nbk, nc=nc),
        out_shape=(jax.ShapeDtypeStruct((proj.shape[0], width), F32),
                   jax.ShapeDtypeStruct((n_rb, g, 2, nbk, p2), F32)),
        grid=(width // lw, n_rb),
        in_specs=in_specs,
        out_specs=(pl.BlockSpec((rb_rows, lw), lambda lt, rb: (rb0 + rb, lt)),
                   pl.BlockSpec((None, gpt, 2, nbk, p2), lambda lt, rb: (rb, lt, 0, 0, 0))),
        scratch_shapes=[pltpu.VMEM((rb_rows, LANES), F32)] * 4 + [pltpu.VMEM((nr, p2), F32)] * 6,
        input_output_aliases=aliases,
        compiler_params=_cparams("arbitrary", "arbitrary"),
        name="s5_scan",
    )(*args)
    fin = jnp.transpose(hfin, (0, 3, 2, 1, 4)).reshape(n_seq, 2, g, p2)
    return y, fin


def _s5_glu_kernel(y_ref, u_ref, d_ref, w_ref, b_ref, o_ref, wb_ref):
    @pl.when(pl.program_id(0) == 0)
    def _():
        wb_ref[...] = w_ref[...].astype(BF16)

    z = _gelu(y_ref[...] + d_ref[...] * u_ref[...])
    gate = jnp.dot(z.astype(BF16), wb_ref[...], preferred_element_type=F32) + b_ref[...]
    o_ref[...] = (z * jax.nn.sigmoid(gate)).astype(o_ref.dtype)


def _s5_glu(y, proj, d_row, w_glu, w_layer, b_row):
    t, w = y.shape
    tm = MM_TM
    return pl.pallas_call(
        _s5_glu_kernel,
        out_shape=jax.ShapeDtypeStruct((t, w), BF16),
        grid=(t // tm,),
        in_specs=[
            pl.BlockSpec((tm, w), lambda i: (i, 0)),
            pl.BlockSpec((tm, w), lambda i: (i, 0)),
            pl.BlockSpec((1, w), lambda i: (0, 0)),
            pl.BlockSpec((None, w, w), lambda i: (w_layer, 0, 0)),
            pl.BlockSpec((1, w), lambda i: (0, 0)),
        ],
        out_specs=pl.BlockSpec((tm, w), lambda i: (i, 0)),
        scratch_shapes=[pltpu.VMEM((w, w), BF16)],
        compiler_params=_cparams("arbitrary"),
        name="s5_glu",
    )(y, proj, d_row, w_glu, b_row)


def _lru_kernel(gate_ref, xr_ref, cw_ref, cb_ref, wa_ref, wx_ref, ba_ref, bx_ref, sp_ref, h0_ref, *rest):
    y_ref, hfin_ref, a_f, b_f, a_b, b_b = rest[-6:]
    seq, lw = xr_ref.shape
    x = xr_ref[...]
    rows = lax.broadcasted_iota(jnp.int32, (seq, lw), 0)
    cw = cw_ref[...]
    xc = cw[2:3] * x + cb_ref[...]
    xc = xc + cw[0:1] * jnp.where(rows >= 2, pltpu.roll(x, 2, 0), 0.0)
    xc = xc + cw[1:2] * jnp.where(rows >= 1, pltpu.roll(x, 1, 0), 0.0)
    xc = xc + cw[3:4] * jnp.where(rows < seq - 1, pltpu.roll(x, seq - 1, 0), 0.0)
    for hb in range(lw // LRU_HEAD_BLOCK):
        lanes = slice(hb * LRU_HEAD_BLOCK, (hb + 1) * LRU_HEAD_BLOCK)
        xb = xc[:, lanes]
        xbb = xb.astype(BF16)
        for d, (a_s, b_s) in enumerate(((a_f, b_f), (a_b, b_b))):
            r = jax.nn.sigmoid(jnp.dot(xbb, wa_ref[d, hb], preferred_element_type=F32) + ba_ref[d:d + 1, lanes])
            gi = jax.nn.sigmoid(jnp.dot(xbb, wx_ref[d, hb], preferred_element_type=F32) + bx_ref[d:d + 1, lanes])
            a = jnp.exp(-LRU_C * r * sp_ref[d:d + 1, lanes])
            a_s[:, lanes] = a
            b_s[:, lanes] = jnp.sqrt(1.0 - a * a) * (gi * xb)

    def step(t, carry):
        hf, hb = carry
        rf = pl.ds(t, 1)
        hf = a_f[rf, :] * hf + b_f[rf, :]
        b_f[rf, :] = hf
        rb = pl.ds(seq - 1 - t, 1)
        hb = a_b[rb, :] * hb + b_b[rb, :]
        b_b[rb, :] = hb
        return hf, hb

    hf, hb = lax.fori_loop(0, seq, step, (h0_ref[0:1, :], h0_ref[1:2, :]), unroll=8)
    hfin_ref[0:1, :] = hf
    hfin_ref[1:2, :] = hb
    y_ref[...] = (_gelu(gate_ref[...]) * (b_f[...] + b_b[...])).astype(y_ref.dtype)


def _lru_mixer(proj, row0, n_seq, seq_len, lane_w, h0, conv_w, conv_b, wa_bd, wx_bd, b_a, b_x, sp, y_prev):
    w = conv_w.shape[1]
    nlb = w // lane_w
    rb0 = row0 // seq_len
    hpb = lane_w // LRU_HEAD_BLOCK
    extra_specs, extra_args, aliases = [], [], {}
    if y_prev is not None:
        extra_specs, extra_args, aliases = [pl.BlockSpec(memory_space=pl.ANY)], [y_prev], {10: 0}
    return pl.pallas_call(
        _lru_kernel,
        out_shape=(jax.ShapeDtypeStruct((proj.shape[0], w), BF16), jax.ShapeDtypeStruct((n_seq, 2, w), F32)),
        grid=(n_seq, nlb),
        input_output_aliases=aliases,
        in_specs=[
            pl.BlockSpec((seq_len, lane_w), lambda b, c: (rb0 + b, nlb + c)),
            pl.BlockSpec((seq_len, lane_w), lambda b, c: (rb0 + b, 2 * nlb + c)),
            pl.BlockSpec((conv_w.shape[0], lane_w), lambda b, c: (0, c)),
            pl.BlockSpec((1, lane_w), lambda b, c: (0, c)),
            pl.BlockSpec((2, hpb, LRU_HEAD_BLOCK, LRU_HEAD_BLOCK), lambda b, c: (0, c, 0, 0)),
            pl.BlockSpec((2, hpb, LRU_HEAD_BLOCK, LRU_HEAD_BLOCK), lambda b, c: (0, c, 0, 0)),
            pl.BlockSpec((2, lane_w), lambda b, c: (0, c)),
            pl.BlockSpec((2, lane_w), lambda b, c: (0, c)),
            pl.BlockSpec((2, lane_w), lambda b, c: (0, c)),
            pl.BlockSpec((None, 2, lane_w), lambda b, c: (b, 0, c)),
        ] + extra_specs,
        out_specs=(pl.BlockSpec((seq_len, lane_w), lambda b, c: (rb0 + b, c)),
                   pl.BlockSpec((None, 2, lane_w), lambda b, c: (b, 0, c))),
        scratch_shapes=[pltpu.VMEM((seq_len, lane_w), F32)] * 4,
        compiler_params=_cparams("arbitrary", "arbitrary"),
        name="rglru",
    )(proj, proj, conv_w, conv_b, wa_bd, wx_bd, b_a, b_x, sp, h0, *extra_args)


def _block_diag_heads(w):
    two, h, hd, _ = w.shape
    per = LRU_HEAD_BLOCK // hd
    wb = w.reshape(two, h // per, per, hd, hd)
    eye = jnp.eye(per, dtype=w.dtype)
    bd = jnp.einsum('dbkij,kl->dbkilj', wb, eye)
    return bd.reshape(two, h // per, LRU_HEAD_BLOCK, LRU_HEAD_BLOCK).astype(BF16)


def _softmax_pv(parts):
    m = None
    for s, _ in parts:
        mm = jnp.max(s, axis=-1, keepdims=True)
        m = mm if m is None else jnp.maximum(m, mm)
    acc, den = None, None
    for s, v in parts:
        p = jnp.exp(s - m)
        l = jnp.sum(p, axis=-1, keepdims=True)
        o = jnp.dot(p.astype(BF16), v, preferred_element_type=F32)
        acc = o if acc is None else acc + o
        den = l if den is None else den + l
    return acc / den


def _qk(q, k):
    return lax.dot_general(q, k, (((1,), (1,)), ((), ())), preferred_element_type=F32)


def _ctx_attn_kernel(q_ref, k_ref, v_ref, *rest, n_heads):
    o_ref, ck_ref, cv_ref = rest[-3:]
    dh = q_ref.shape[1] // n_heads
    scale = dh ** -0.5
    ck_ref[...] = k_ref[...]
    cv_ref[...] = v_ref[...]
    for h in range(n_heads):
        lanes = slice(h * dh, (h + 1) * dh)
        q = q_ref[:, lanes].astype(BF16)
        k = k_ref[:, lanes].astype(BF16)
        v = v_ref[:, lanes].astype(BF16)
        o_ref[:, lanes] = _softmax_pv([(_qk(q, k) * scale, v)]).astype(o_ref.dtype)


def _ctx_attention(qkv, n_seq, seq_len, n_heads, total_rows, layer_j, n_attn_layers, caches):
    d = qkv.shape[1] // 3
    cache_shape = jax.ShapeDtypeStruct((n_seq, n_attn_layers, seq_len, d), F32)
    cache_spec = pl.BlockSpec((None, None, seq_len, d), lambda b: (b, layer_j, 0, 0))
    in_specs = [pl.BlockSpec((seq_len, d), lambda b, cb=cb: (b, cb)) for cb in range(3)]
    args = [qkv, qkv, qkv]
    aliases = {}
    if caches is not None:
        in_specs += [pl.BlockSpec(memory_space=pl.ANY)] * 2
        args += list(caches)
        aliases = {3: 1, 4: 2}
    return pl.pallas_call(
        functools.partial(_ctx_attn_kernel, n_heads=n_heads),
        out_shape=(jax.ShapeDtypeStruct((total_rows, d), BF16), cache_shape, cache_shape),
        grid=(n_seq,),
        in_specs=in_specs,
        out_specs=(pl.BlockSpec((seq_len, d), lambda b: (b, 0)), cache_spec, cache_spec),
        input_output_aliases=aliases,
        compiler_params=_cparams("arbitrary"),
        name="ctx_attention",
    )(*args)


def _nbr_row_windows(rows):
    kr = min(WIN_ROWS_MAX, rows)
    starts = np.clip(np.arange(rows) - kr // 2, 0, rows - kr)
    groups, r = [], 0
    while r < rows:
        r1 = r
        while r1 < rows and starts[r1] == starts[r]:
            r1 += 1
        groups.append((r, r1, int(starts[r])))
        r = r1
    return kr, starts, groups


def _nbr_attn_kernel(q_ref, k_ref, v_ref, kc_ref, vc_ref, bias_ref, o_in_ref, o_ref):
    del o_in_ref
    dh = q_ref.shape[1]
    scale = dh ** -0.5
    kr, starts, groups = _nbr_row_windows(q_ref.shape[0] // GRID_W)
    k_all = k_ref[...].astype(BF16)
    v_all = v_ref[...].astype(BF16)
    kc = kc_ref[...].astype(BF16)
    vc = vc_ref[...].astype(BF16)
    for r0, r1, rs in groups:
        q_rows = slice(r0 * GRID_W, r1 * GRID_W)
        k_rows = slice(rs * GRID_W, (rs + kr) * GRID_W)
        q = q_ref[q_rows, :].astype(BF16)
        bias = jnp.concatenate([bias_ref[int(starts[r]) - r + WIN_ROWS_MAX - 1] for r in range(r0, r1)], axis=0)
        s_loc = _qk(q, k_all[k_rows]) * scale + bias
        s_ctx = _qk(q, kc) * scale
        o_ref[q_rows, :] = _softmax_pv([(s_loc, v_all[k_rows]), (s_ctx, vc)]).astype(o_ref.dtype)


def _nbr_attention(qkv, o_all, cache_k, cache_v, layer_j, bias, row0, n_seq, seq_len, n_heads):
    d = qkv.shape[1] // 3
    dh = d // n_heads
    rb0 = row0 // seq_len
    past = cache_k.shape[2]
    ck = cache_k.reshape(cache_k.shape[0], cache_k.shape[1], past, d)
    cv = cache_v.reshape(ck.shape)
    cache_spec = pl.BlockSpec((None, None, past, dh), lambda h, b: (b, layer_j, 0, h))
    return pl.pallas_call(
        _nbr_attn_kernel,
        out_shape=jax.ShapeDtypeStruct(o_all.shape, o_all.dtype),
        grid=(n_heads, n_seq),
        in_specs=[
            pl.BlockSpec((seq_len, dh), lambda h, b: (rb0 + b, h)),
            pl.BlockSpec((seq_len, dh), lambda h, b: (rb0 + b, n_heads + h)),
            pl.BlockSpec((seq_len, dh), lambda h, b: (rb0 + b, 2 * n_heads + h)),
            cache_spec, cache_spec,
            pl.BlockSpec((None,) + bias.shape[1:], lambda h, b: (h, 0, 0, 0)),
            pl.BlockSpec(memory_space=pl.ANY),
        ],
        out_specs=pl.BlockSpec((seq_len, dh), lambda h, b: (rb0 + b, h)),
        input_output_aliases={6: 0},
        compiler_params=_cparams("arbitrary", "arbitrary"),
        name="nbr_attention",
    )(qkv, qkv, qkv, ck, cv, bias, o_all)


def _nbr_bias(rpb, rows):
    kr, _, _ = _nbr_row_windows(rows)
    c_idx = np.arange(GRID_W)
    col_start = np.clip(c_idx - WIN_COLS // 2, 0, GRID_W - WIN_COLS)
    kcol = np.arange(GRID_W)[None, :]
    col_valid = (kcol >= col_start[:, None]) & (kcol < col_start[:, None] + WIN_COLS)
    col_off = np.clip(kcol - c_idx[:, None] + WIN_COLS - 1, 0, 2 * WIN_COLS - 2)
    col_sel = jnp.asarray(np.eye(2 * WIN_COLS - 1, dtype=np.float32)[col_off])
    tab = jnp.einsum('hij,cmj->hcim', rpb.astype(F32), col_sel, precision=lax.Precision.HIGHEST)
    tab = jnp.where(jnp.asarray(col_valid)[None, :, None, :], tab, NEG_INF)
    n_win = 2 * WIN_ROWS_MAX - kr
    wins = [tab[:, :, i0:i0 + kr, :].reshape(tab.shape[0], GRID_W, kr * GRID_W) for i0 in range(n_win)]
    return jnp.stack(wins, axis=1)


def _row_copy(hbm, row, buf, slot, r, sem):
    return pltpu.make_async_copy(hbm.at[pl.ds(row, 1), :], buf.at[slot, pl.ds(r, 1), :], sem.at[slot])


def _moe_gather_kernel(src_ref, nv_ref, h_hbm, o_ref, buf, sem):
    i = pl.program_id(0)
    tm = o_ref.shape[0]
    slot = i % 2

    def issue_tile(tile, s):
        base = tile * tm

        def issue(q, c):
            for par in range(N_DMA_PRIORITIES):
                r = q * N_DMA_PRIORITIES + par
                _row_copy(h_hbm, src_ref[base + r], buf, s, r, sem).start(priority=par)
            return c

        lax.fori_loop(0, tm // N_DMA_PRIORITIES, issue, 0, unroll=4)

    @pl.when(i == 0)
    def _():
        issue_tile(0, 0)

    @pl.when(i + 1 < nv_ref[0])
    def _():
        issue_tile(i + 1, 1 - slot)

    @pl.when(i < nv_ref[0])
    def _():
        pltpu.make_async_copy(h_hbm.at[pl.ds(0, tm), :], buf.at[slot], sem.at[slot]).wait()
        o_ref[...] = buf[slot].astype(o_ref.dtype)

    @pl.when(i >= nv_ref[0])
    def _():
        o_ref[...] = jnp.zeros_like(o_ref)


def _moe_gather(h, src, n_valid, n_rows):
    tm = MOE_TM
    d = h.shape[1]
    return pl.pallas_call(
        _moe_gather_kernel,
        out_shape=jax.ShapeDtypeStruct((n_rows, d), BF16),
        grid_spec=pltpu.PrefetchScalarGridSpec(
            num_scalar_prefetch=2,
            grid=(n_rows // tm,),
            in_specs=[pl.BlockSpec(memory_space=pl.ANY)],
            out_specs=pl.BlockSpec((tm, d), lambda i, src, nv: (i, 0)),
            scratch_shapes=[pltpu.VMEM((2, tm, d), F32), pltpu.SemaphoreType.DMA((2,))],
        ),
        compiler_params=_cparams("arbitrary"),
        name="moe_gather",
    )(src, n_valid, h)


def _new_expert(te_ref, i):
    return (i == 0) | (te_ref[i] != te_ref[jnp.maximum(i - 1, 0)])


def _moe_gu_kernel(te_ref, nv_ref, x_ref, wg_ref, wu_ref, o_ref, wgb, wub):
    i = pl.program_id(1)
    valid = i < nv_ref[0]

    @pl.when(valid & _new_expert(te_ref, i))
    def _():
        wgb[...] = wg_ref[...].astype(BF16)
        wub[...] = wu_ref[...].astype(BF16)

    @pl.when(valid)
    def _():
        x = x_ref[...]
        a = jnp.dot(x, wgb[...], preferred_element_type=F32)
        b = jnp.dot(x, wub[...], preferred_element_type=F32)
        o_ref[...] = (a * jax.nn.sigmoid(a) * b).astype(o_ref.dtype)

    @pl.when(jnp.logical_not(valid))
    def _():
        o_ref[...] = jnp.zeros_like(o_ref)


def _moe_gate_up(xs, w_gu, layer, tile_expert, n_valid):
    p, d = xs.shape
    de = w_gu.shape[3] // 2
    tm, tn = MOE_TM, min(MOE_TN, de)
    nj = de // tn
    row = lambda i, nv: jnp.minimum(i, nv[0] - 1)
    return pl.pallas_call(
        _moe_gu_kernel,
        out_shape=jax.ShapeDtypeStruct((p, de), BF16),
        grid_spec=pltpu.PrefetchScalarGridSpec(
            num_scalar_prefetch=2,
            grid=(nj, p // tm),
            in_specs=[
                pl.BlockSpec((tm, d), lambda j, i, te, nv: (row(i, nv), 0)),
                pl.BlockSpec((None, None, d, tn), lambda j, i, te, nv: (layer, te[i], 0, j)),
                pl.BlockSpec((None, None, d, tn), lambda j, i, te, nv: (layer, te[i], 0, nj + j)),
            ],
            out_specs=pl.BlockSpec((tm, tn), lambda j, i, te, nv: (i, j)),
            scratch_shapes=[pltpu.VMEM((d, tn), BF16), pltpu.VMEM((d, tn), BF16)],
        ),
        compiler_params=_cparams("arbitrary", "arbitrary"),
        name="moe_gate_up",
    )(tile_expert, n_valid, xs, w_gu, w_gu)


def _moe_dn_kernel(te_ref, nv_ref, h_ref, w_ref, o_ref, wb):
    i = pl.program_id(0)
    valid = i < nv_ref[0]

    @pl.when(valid & _new_expert(te_ref, i))
    def _():
        wb[...] = w_ref[...].astype(BF16)

    @pl.when(valid)
    def _():
        o_ref[...] = jnp.dot(h_ref[...], wb[...], preferred_element_type=F32)

    @pl.when(jnp.logical_not(valid))
    def _():
        o_ref[...] = jnp.zeros_like(o_ref)


def _moe_down(hmid, w_dn, layer, tile_expert, n_valid):
    p, de = hmid.shape
    d = w_dn.shape[3]
    tm = MOE_TM
    row = lambda i, nv: jnp.minimum(i, nv[0] - 1)
    return pl.pallas_call(
        _moe_dn_kernel,
        out_shape=jax.ShapeDtypeStruct((p, d), F32),
        grid_spec=pltpu.PrefetchScalarGridSpec(
            num_scalar_prefetch=2,
            grid=(p // tm,),
            in_specs=[
                pl.BlockSpec((tm, de), lambda i, te, nv: (row(i, nv), 0)),
                pl.BlockSpec((None, None, de, d), lambda i, te, nv: (layer, te[i], 0, 0)),
            ],
            out_specs=pl.BlockSpec((tm, d), lambda i, te, nv: (i, 0)),
            scratch_shapes=[pltpu.VMEM((de, d), BF16)],
        ),
        compiler_params=_cparams("arbitrary"),
        name="moe_down",
    )(tile_expert, n_valid, hmid, w_dn)


def _moe_combine_kernel(p1_ref, p2_ref, y_hbm, x_ref, w_ref, g_ref, o_ref, buf1, buf2, sem):
    i = pl.program_id(0)
    tm = o_ref.shape[0]
    slot = i % 2

    def issue_tile(tile, s):
        base = tile * tm

        def issue(r, c):
            _row_copy(y_hbm, p1_ref[base + r], buf1, s, r, sem).start(priority=0)
            _row_copy(y_hbm, p2_ref[base + r], buf2, s, r, sem).start(priority=1)
            return c

        lax.fori_loop(0, tm, issue, 0, unroll=8)

    @pl.when(i == 0)
    def _():
        issue_tile(0, 0)

    @pl.when(i + 1 < pl.num_programs(0))
    def _():
        issue_tile(i + 1, 1 - slot)

    pltpu.make_async_copy(y_hbm.at[pl.ds(0, tm), :], buf1.at[slot], sem.at[slot]).wait()
    pltpu.make_async_copy(y_hbm.at[pl.ds(0, tm), :], buf2.at[slot], sem.at[slot]).wait()
    w = w_ref[...]
    y = w[:, 0:1] * buf1[slot] + w[:, 1:2] * buf2[slot]
    o_ref[...] = x_ref[...] + g_ref[...] * y


def _moe_combine(y_sorted, x, wts, p1, p2, mods3, layer, lay):
    t, d = x.shape
    tm = ROW_TILE
    return pl.pallas_call(
        _moe_combine_kernel,
        out_shape=jax.ShapeDtypeStruct((t, d), F32),
        grid_spec=pltpu.PrefetchScalarGridSpec(
            num_scalar_prefetch=2,
            grid=(t // tm,),
            in_specs=[
                pl.BlockSpec(memory_space=pl.ANY),
                pl.BlockSpec((tm, d), lambda i, p1, p2: (i, 0)),
                pl.BlockSpec((tm, wts.shape[1]), lambda i, p1, p2: (i, 0)),
                pl.BlockSpec((None, 1, d), lambda i, p1, p2: (_mod_index(layer, 5, lay.mod_row(i, tm)), 0, 0)),
            ],
            out_specs=pl.BlockSpec((tm, d), lambda i, p1, p2: (i, 0)),
            scratch_shapes=[pltpu.VMEM((2, tm, d), F32), pltpu.VMEM((2, tm, d), F32), pltpu.SemaphoreType.DMA((2,))],
        ),
        compiler_params=_cparams("arbitrary"),
        name="moe_combine",
    )(p1, p2, y_sorted, x, wts, mods3)


def _moe_plan(route, n_experts):
    t = route.shape[1]
    tm = MOE_TM
    eid = route[0:2].astype(jnp.int32).reshape(-1)
    onehot = (eid[:, None] == jnp.arange(n_experts, dtype=jnp.int32)[None, :]).astype(jnp.int32)
    csum = jnp.cumsum(onehot, axis=0)
    rank = jnp.take_along_axis(csum, eid[:, None], axis=1)[:, 0] - 1
    counts = csum[-1]
    padded = ((counts + tm - 1) // tm) * tm
    ends = jnp.cumsum(padded)
    offs = ends - padded
    pos = offs[eid] + rank
    n_rows = 2 * t + n_experts * tm
    tok = jnp.tile(jnp.arange(t, dtype=jnp.int32), 2)
    src = (jnp.arange(n_rows, dtype=jnp.int32) % t).at[pos].set(tok)
    n_tiles = n_rows // tm
    n_valid = (ends[-1] // tm).astype(jnp.int32)
    starts = jnp.arange(n_tiles, dtype=jnp.int32) * tm
    te = jnp.sum((starts[:, None] >= ends[None, :]).astype(jnp.int32), axis=1)
    te_last = jnp.sum((((n_valid - 1) * tm) >= ends).astype(jnp.int32))
    te = jnp.where(starts < ends[-1], te, te_last).astype(jnp.int32)
    wts = jnp.transpose(route[2:4])
    wts = jnp.pad(wts, ((0, 0), (0, 6)))
    return src, te, n_valid.reshape(1), pos[:t], pos[t:], wts, n_rows


def _moe_layer(x, h2, route, mods3, layer, lay, w_gu, w_dn):
    n_experts = w_gu.shape[1]
    src, te, n_valid, p1, p2, wts, n_rows = _moe_plan(route, n_experts)
    xs = _moe_gather(h2, src, n_valid, n_rows)
    hmid = _moe_gate_up(xs, w_gu, layer, te, n_valid)
    ys = _moe_down(hmid, w_dn, layer, te, n_valid)
    return _moe_combine(ys, x, wts, p1, p2, mods3, layer, lay)


def kernel(x_prompt, x_sample, state_s5_re, state_s5_im, state_lru, cache_attn_k, cache_attn_v, c, c_ctx, w_ada, b_ada, norm1_g, norm2_g, final_norm_g, w_in_even, w_out_even, s5_lam_re, s5_lam_im, s5_log_dt, s5_b_re, s5_b_im, s5_c_re, s5_c_im, s5_d, s5_w_glu, s5_b_glu, lru_conv_w, lru_conv_b, lru_w_a, lru_b_a, lru_w_x, lru_b_x, lru_lam, w_qkv, w_o, rpb, w_router, b_router, w_gate_up, w_down):
    batch, seq, d = x_prompt.shape
    dec_batch, dec_seq, _ = x_sample.shape
    depth = w_ada.shape[0]
    n_heads = cache_attn_k.shape[3]
    s5_w = s5_d.shape[1]
    lru_w = lru_conv_w.shape[2]
    n_groups, n_state = s5_lam_re.shape[2], s5_lam_re.shape[3]
    assert dec_batch < MOD_ROWS
    n_ctx = batch * seq
    assert n_ctx % MM_TM == 0 and dec_seq % MM_TM == 0 and seq % ROW_TILE == 0 and dec_seq % ROW_TILE == 0
    lay = _Layout(n_ctx, dec_batch, dec_seq)
    t = lay.total

    x = jnp.concatenate([x_prompt.reshape(n_ctx, d), x_sample.reshape(dec_batch * dec_seq, d)], axis=0)
    cvec = jnp.zeros((MOD_ROWS, d), F32).at[:dec_batch].set(c).at[dec_batch].set(c_ctx)
    mods = _ada_project(cvec, w_ada, b_ada)
    mods3 = mods.reshape(depth * MOD_ROWS * N_MOD, 1, d)
    w_router_t = jnp.transpose(w_router)
    b_router_col = b_router.reshape(-1, 1)

    s5_re_list, s5_im_list, lru_list = [], [], []
    kv_caches = None
    for l in range(depth):
        j = l // 2
        h1 = _norm_mod(x, norm1_g[l].reshape(1, d), mods3, l, 0, lay, BF16)
        if l % 2 == 0:
            proj = _matmul([h1], w_in_even, j, F32)
            mats = _s5_matrices(s5_lam_re[j], s5_lam_im[j], s5_log_dt[j], s5_b_re[j], s5_b_im[j],
                                s5_c_re[j], s5_c_im[j])
            zero_h0 = jnp.zeros((batch, 2, n_groups, 2 * n_state), F32)
            lat_h0 = jnp.concatenate([state_s5_re[:, j], state_s5_im[:, j]], axis=-1).astype(F32)
            y_scan, s5_fin = _s5_scan(proj, 0, zero_h0, mats, batch, seq, s5_w, None)
            y_scan, _ = _s5_scan(proj, n_ctx, lat_h0, mats, dec_batch, dec_seq, s5_w, y_scan)
            s5_re_list.append(s5_fin[..., :n_state])
            s5_im_list.append(s5_fin[..., n_state:])
            y_s5 = _s5_glu(y_scan, proj, s5_d[j].reshape(1, s5_w), s5_w_glu, j, s5_b_glu[j].reshape(1, s5_w))
            wa_bd = _block_diag_heads(lru_w_a[j])
            wx_bd = _block_diag_heads(lru_w_x[j])
            sp = jax.nn.softplus(-lru_lam[j].astype(F32))
            lru_args = (lru_conv_w[j], lru_conv_b[j].reshape(1, lru_w), wa_bd, wx_bd, lru_b_a[j], lru_b_x[j], sp)
            y_lru, lru_fin = _lru_mixer(proj, 0, batch, seq, lru_w, jnp.zeros((batch, 2, lru_w), F32),
                                        *lru_args, None)
            y_lru, _ = _lru_mixer(proj, n_ctx, dec_batch, dec_seq, lru_w // 2, state_lru[:, j].astype(F32),
                                  *lru_args, y_lru)
            lru_list.append(lru_fin)
            x = _matmul([y_s5, y_lru], w_out_even, j, F32, resid=(x, mods3, l, 2, lay))
        else:
            qkv = _matmul([h1], w_qkv, j, F32)
            o_all, *kv_caches = _ctx_attention(qkv, batch, seq, n_heads, t, j, depth // 2, kv_caches)
            bias = _nbr_bias(rpb[j], dec_seq // GRID_W)
            o_all = _nbr_attention(qkv, o_all, cache_attn_k, cache_attn_v, j, bias, n_ctx, dec_batch, dec_seq, n_heads)
            x = _matmul([o_all], w_o, j, F32, resid=(x, mods3, l, 2, lay))
        h2, route = _norm_mod_route(x, norm2_g[l].reshape(1, d), mods3, l, lay, w_router_t, b_router_col)
        x = _moe_layer(x, h2, route, mods3, l, lay, w_gate_up, w_down)

    g_fin = final_norm_g.reshape(1, d)
    y_prompt = _final_norm(x, g_fin, 0, n_ctx).reshape(batch, seq, d)
    y_sample = _final_norm(x, g_fin, n_ctx, t - n_ctx).reshape(dec_batch, dec_seq, d)
    cache_shape = (batch, depth // 2, seq, n_heads, d // n_heads)
    return (y_prompt, y_sample, jnp.stack(s5_re_list, axis=1), jnp.stack(s5_im_list, axis=1),
            jnp.stack(lru_list, axis=1), kv_caches[0].reshape(cache_shape), kv_caches[1].reshape(cache_shape))
```

```python
import functools
import math

import numpy as np
import jax
import jax.numpy as jnp
from jax import lax
from jax.experimental import pallas as pl
from jax.experimental.pallas import tpu as pltpu

F32 = jnp.float32
BF16 = jnp.bfloat16

LANES = 128
N_DMA_PRIORITIES = 2
NORM_EPS = 1e-6
NEG_INF = -1e30
S5_GROUP = 16
S5_CHUNK = 16
S5_LANE_TILE = 256
S5_ROW_BLOCK = 2048
S5_TRANSPOSE_ROWS = 64
S5_TRANSPOSE_UNROLL = 4
S5_GROUPS_IN_FLIGHT = 2
LRU_C = 8.0
LRU_HEAD_BLOCK = 256
N_EXPERT_GROUPS = 4
WIN_ROWS_MAX = 8
WIN_COLS = 16
GRID_W = 64
N_MOD = 6
MOD_ROWS = 8
VMEM_LIMIT = 52 * 1024 * 1024
ROW_TILE = 256
MM_TM = 1024
MM_TN = 1024
MM_TN_RESID = 512
MOE_TM = 512
MOE_TN = 512


def _cparams(*sem):
    return pltpu.CompilerParams(dimension_semantics=sem, vmem_limit_bytes=VMEM_LIMIT)


def _gelu(x):
    return 0.5 * x * (1.0 + jnp.tanh(math.sqrt(2.0 / math.pi) * (x + 0.044715 * (x * x * x))))


class _Layout:
    def __init__(self, n_ctx_rows, dec_batch, dec_seq):
        self.n_ctx = n_ctx_rows
        self.dec_batch = dec_batch
        self.dec_seq = dec_seq
        self.total = n_ctx_rows + dec_batch * dec_seq

    def mod_row(self, i, tm):
        nct = self.n_ctx // tm
        per = self.dec_seq // tm
        return jnp.where(i < nct, self.dec_batch, (i - nct) // per)


def _mod_index(layer, which, row):
    return (layer * MOD_ROWS + row) * N_MOD + which


def _ada_kernel(c_ref, w_ref, b_ref, o_ref):
    c = c_ref[...]
    s = (c * jax.nn.sigmoid(c)).astype(BF16)
    o_ref[...] = jnp.dot(s, w_ref[...].astype(BF16), preferred_element_type=F32) + b_ref[...]


def _ada_project(cvec, w_ada, b_ada):
    depth, d, n = w_ada.shape
    tn = MM_TN
    return pl.pallas_call(
        _ada_kernel,
        out_shape=jax.ShapeDtypeStruct((depth, MOD_ROWS, n), F32),
        grid=(depth, n // tn),
        in_specs=[
            pl.BlockSpec((MOD_ROWS, d), lambda l, j: (0, 0)),
            pl.BlockSpec((None, d, tn), lambda l, j: (l, 0, j)),
            pl.BlockSpec((None, 1, tn), lambda l, j: (l, 0, j)),
        ],
        out_specs=pl.BlockSpec((None, MOD_ROWS, tn), lambda l, j: (l, 0, j)),
        compiler_params=_cparams("arbitrary", "arbitrary"),
        name="ada_project",
    )(cvec, w_ada, b_ada.reshape(depth, 1, n))


def _rms(x, g):
    ms = jnp.mean(x * x, axis=-1, keepdims=True)
    return x * lax.rsqrt(ms + NORM_EPS) * g


def _norm_mod_kernel(x_ref, g_ref, sh_ref, sc_ref, o_ref):
    y = _rms(x_ref[...], g_ref[...])
    o_ref[...] = (y * (1.0 + sc_ref[...]) + sh_ref[...]).astype(o_ref.dtype)


def _norm_mod(x, g_row, mods3, layer, which_shift, lay, out_dtype):
    t, d = x.shape
    tm = ROW_TILE
    row = lambda i: lay.mod_row(i, tm)
    return pl.pallas_call(
        _norm_mod_kernel,
        out_shape=jax.ShapeDtypeStruct((t, d), out_dtype),
        grid=(t // tm,),
        in_specs=[
            pl.BlockSpec((tm, d), lambda i: (i, 0)),
            pl.BlockSpec((1, d), lambda i: (0, 0)),
            pl.BlockSpec((None, 1, d), lambda i: (_mod_index(layer, which_shift, row(i)), 0, 0)),
            pl.BlockSpec((None, 1, d), lambda i: (_mod_index(layer, which_shift + 1, row(i)), 0, 0)),
        ],
        out_specs=pl.BlockSpec((tm, d), lambda i: (i, 0)),
        compiler_params=_cparams("arbitrary"),
        name="norm_mod",
    )(x, g_row, mods3, mods3)


def _top2_of4(a):
    m1 = jnp.maximum(jnp.maximum(a[0], a[1]), jnp.maximum(a[2], a[3]))
    i1 = jnp.where(a[0] == m1, 0, jnp.where(a[1] == m1, 1, jnp.where(a[2] == m1, 2, 3)))
    b = [jnp.where(i1 == k, -jnp.inf, a[k]) for k in range(4)]
    m2 = jnp.maximum(jnp.maximum(b[0], b[1]), jnp.maximum(b[2], b[3]))
    i2 = jnp.where(b[0] == m2, 0, jnp.where(b[1] == m2, 1, jnp.where(b[2] == m2, 2, 3)))
    return m1 + m2, i1, i2


def _norm_mod_route_kernel(x_ref, g_ref, sh_ref, sc_ref, wr_ref, br_ref, o_ref, r_ref):
    y = _rms(x_ref[...], g_ref[...])
    h = y * (1.0 + sc_ref[...]) + sh_ref[...]
    o_ref[...] = h
    logits = lax.dot_general(wr_ref[...], h, (((1,), (1,)), ((), ())),
                             precision=lax.Precision.HIGHEST, preferred_element_type=F32)
    scores = jax.nn.sigmoid(logits)
    sel = scores + br_ref[...]
    n_e = scores.shape[0]
    per = n_e // N_EXPERT_GROUPS
    sel_rows = [sel[e:e + 1, :] for e in range(n_e)]
    score_rows = [scores[e:e + 1, :] for e in range(n_e)]
    gs, i1s, i2s = [], [], []
    for gi in range(N_EXPERT_GROUPS):
        s, i1, i2 = _top2_of4(sel_rows[gi * per:(gi + 1) * per])
        gs.append(s)
        i1s.append(i1)
        i2s.append(i2)
    gmax = jnp.maximum(jnp.maximum(gs[0], gs[1]), jnp.maximum(gs[2], gs[3]))
    gsel = jnp.where(gs[0] == gmax, 0, jnp.where(gs[1] == gmax, 1, jnp.where(gs[2] == gmax, 2, 3)))
    l1 = jnp.where(gsel == 0, i1s[0], jnp.where(gsel == 1, i1s[1], jnp.where(gsel == 2, i1s[2], i1s[3])))
    l2 = jnp.where(gsel == 0, i2s[0], jnp.where(gsel == 1, i2s[1], jnp.where(gsel == 2, i2s[2], i2s[3])))
    e1 = gsel * per + l1
    e2 = gsel * per + l2
    w1 = jnp.zeros_like(gmax)
    w2 = jnp.zeros_like(gmax)
    for e in range(n_e):
        w1 = jnp.where(e1 == e, score_rows[e], w1)
        w2 = jnp.where(e2 == e, score_rows[e], w2)
    wsum = w1 + w2
    zero = jnp.zeros_like(gmax)
    r_ref[...] = jnp.concatenate(
        [e1.astype(F32), e2.astype(F32), w1 / wsum, w2 / wsum, zero, zero, zero, zero], axis=0)


def _norm_mod_route(x, g_row, mods3, layer, lay, w_router_t, b_router_col):
    t, d = x.shape
    tm = ROW_TILE
    n_e = w_router_t.shape[0]
    row = lambda i: lay.mod_row(i, tm)
    return pl.pallas_call(
        _norm_mod_route_kernel,
        out_shape=(jax.ShapeDtypeStruct((t, d), F32), jax.ShapeDtypeStruct((8, t), F32)),
        grid=(t // tm,),
        in_specs=[
            pl.BlockSpec((tm, d), lambda i: (i, 0)),
            pl.BlockSpec((1, d), lambda i: (0, 0)),
            pl.BlockSpec((None, 1, d), lambda i: (_mod_index(layer, 3, row(i)), 0, 0)),
            pl.BlockSpec((None, 1, d), lambda i: (_mod_index(layer, 4, row(i)), 0, 0)),
            pl.BlockSpec((n_e, d), lambda i: (0, 0)),
            pl.BlockSpec((n_e, 1), lambda i: (0, 0)),
        ],
        out_specs=(pl.BlockSpec((tm, d), lambda i: (i, 0)), pl.BlockSpec((8, tm), lambda i: (0, i))),
        compiler_params=_cparams("arbitrary"),
        name="norm_mod_route",
    )(x, g_row, mods3, mods3, w_router_t, b_router_col)


def _final_norm_kernel(x_ref, g_ref, o_ref):
    o_ref[...] = _rms(x_ref[...], g_ref[...])


def _final_norm(x, g_row, row0, n_rows):
    d = x.shape[1]
    tm = ROW_TILE
    return pl.pallas_call(
        _final_norm_kernel,
        out_shape=jax.ShapeDtypeStruct((n_rows, d), F32),
        grid=(n_rows // tm,),
        in_specs=[pl.BlockSpec((tm, d), lambda i: (row0 // tm + i, 0)), pl.BlockSpec((1, d), lambda i: (0, 0))],
        out_specs=pl.BlockSpec((tm, d), lambda i: (i, 0)),
        compiler_params=_cparams("arbitrary"),
        name="final_norm",
    )(x, g_row)


def _mm_kernel(*refs, n_a, resid):
    a_refs = refs[:n_a]
    w_ref = refs[n_a]
    pos = n_a + 1
    if resid:
        x_ref, g_ref = refs[pos], refs[pos + 1]
        pos += 2
    o_ref, wb_ref = refs[pos], refs[pos + 1]

    @pl.when(pl.program_id(1) == 0)
    def _():
        wb_ref[...] = w_ref[...].astype(BF16)

    acc = None
    k0 = 0
    for a_ref in a_refs:
        ka = a_ref.shape[1]
        part = jnp.dot(a_ref[...], wb_ref[k0:k0 + ka, :], preferred_element_type=F32)
        acc = part if acc is None else acc + part
        k0 += ka
    if resid:
        o_ref[...] = x_ref[...] + g_ref[...] * acc
    else:
        o_ref[...] = acc.astype(o_ref.dtype)


def _matmul(a_list, w, w_layer, out_dtype, resid=None):
    m = a_list[0].shape[0]
    _, k, n = w.shape
    tm = MM_TM
    tn = min(MM_TN_RESID if resid is not None else MM_TN, n)
    in_specs = [pl.BlockSpec((tm, a.shape[1]), lambda j, i: (i, 0)) for a in a_list]
    in_specs.append(pl.BlockSpec((None, k, tn), lambda j, i: (w_layer, 0, j)))
    args = list(a_list) + [w]
    if resid is not None:
        x, mods3, layer, which, lay = resid
        in_specs.append(pl.BlockSpec((tm, tn), lambda j, i: (i, j)))
        in_specs.append(pl.BlockSpec(
            (None, 1, tn), lambda j, i: (_mod_index(layer, which, lay.mod_row(i, tm)), 0, j)))
        args += [x, mods3]
    return pl.pallas_call(
        functools.partial(_mm_kernel, n_a=len(a_list), resid=resid is not None),
        out_shape=jax.ShapeDtypeStruct((m, n), out_dtype),
        grid=(n // tn, m // tm),
        in_specs=in_specs,
        out_specs=pl.BlockSpec((tm, tn), lambda j, i: (i, j)),
        scratch_shapes=[pltpu.VMEM((k, tn), BF16)],
        compiler_params=_cparams("arbitrary", "arbitrary"),
        name="matmul_resid" if resid is not None else "matmul",
    )(*args)


def _cmul(ar, ai, br, bi):
    return ar * br - ai * bi, ar * bi + ai * br


def _s5_matrices(lam_re, lam_im, log_dt, b_re, b_im, c_re, c_im):
    s = S5_CHUNK
    f = lambda z: z.astype(F32)
    lam_re, lam_im, log_dt, b_re, b_im, c_re, c_im = map(f, (lam_re, lam_im, log_dt, b_re, b_im, c_re, c_im))
    dt = jnp.exp(log_dt)[..., None]
    kk = jnp.arange(s + 1, dtype=F32)[:, None, None, None]
    mag = jnp.exp(kk * (lam_re * dt)[None])
    ph = kk * (lam_im * dt)[None]
    pw_re, pw_im = mag * jnp.cos(ph), mag * jnp.sin(ph)
    a_re, a_im = pw_re[1], pw_im[1]
    den = lam_re * lam_re + lam_im * lam_im
    q_re, q_im = _cmul(a_re - 1.0, a_im, lam_re / den, -lam_im / den)
    bb_re, bb_im = _cmul(q_re[..., None], q_im[..., None], b_re, b_im)
    cp_re, cp_im = _cmul(c_re[:, :, None], c_im[:, :, None],
                         jnp.moveaxis(pw_re, 0, 2)[:, :, :, None, :], jnp.moveaxis(pw_im, 0, 2)[:, :, :, None, :])
    bt_re = jnp.swapaxes(bb_re, 2, 3)[:, :, None, None]
    bt_im = jnp.swapaxes(bb_im, 2, 3)[:, :, None, None]
    kern = jnp.sum(cp_re[:, :, :, :, None, :] * bt_re - cp_im[:, :, :, :, None, :] * bt_im, axis=-1)
    sp = np.arange(s)[:, None]
    so = np.arange(s)[None, :]
    lag_f = np.clip(so - sp, 0, s)
    lag_b = np.clip(sp - so, 0, s)
    kf = jnp.where(jnp.asarray(so >= sp)[None, :, :, None, None], kern[0][:, lag_f], 0.0)
    kb = jnp.where(jnp.asarray(sp >= so)[None, :, :, None, None], kern[1][:, lag_b], 0.0)
    g = kern.shape[1]
    n = s * S5_GROUP
    tmat = jnp.transpose(kf + kb, (0, 1, 4, 2, 3)).reshape(g, n, n)

    def e_mat(d, powers):
        pr = jnp.moveaxis(pw_re[powers, d], 0, 1)[:, :, :, None]
        pi = jnp.moveaxis(pw_im[powers, d], 0, 1)[:, :, :, None]
        er, ei = _cmul(pr, pi, bb_re[d][:, None], bb_im[d][:, None])
        er = jnp.transpose(er, (0, 1, 3, 2)).reshape(g, n, -1)
        ei = jnp.transpose(ei, (0, 1, 3, 2)).reshape(g, n, -1)
        return jnp.concatenate([er, ei], axis=-1)

    def c_mat(d, powers):
        cr = jnp.transpose(cp_re[d][:, powers], (0, 3, 1, 2)).reshape(g, -1, n)
        ci = jnp.transpose(cp_im[d][:, powers], (0, 3, 1, 2)).reshape(g, -1, n)
        return jnp.concatenate([cr, -ci], axis=1)

    swap = lambda e: jnp.concatenate([e[..., e.shape[-1] // 2:], e[..., :e.shape[-1] // 2]], axis=-1)
    e_f = e_mat(0, np.arange(s - 1, -1, -1))
    e_b = e_mat(1, np.arange(s))
    e_all = jnp.concatenate([e_f, swap(e_f), e_b, swap(e_b)], axis=-1)
    c_all = jnp.concatenate([c_mat(0, np.arange(1, s + 1)), c_mat(1, np.arange(s, 0, -1))], axis=1)
    dec_r = jnp.concatenate([pw_re[s], pw_re[s]], axis=-1)
    dec_i = jnp.concatenate([-pw_im[s], pw_im[s]], axis=-1)
    decay = jnp.stack([dec_r, dec_i], axis=2)[:, :, :, None, :]
    return tmat.astype(BF16), e_all.astype(BF16), c_all.astype(BF16), decay


def _block_transpose(v):
    n, w = v.shape
    rows = lax.broadcasted_iota(jnp.int32, v.shape, 0)
    lanes = lax.broadcasted_iota(jnp.int32, v.shape, 1)
    k = S5_CHUNK // 2
    while k >= 1:
        up = pltpu.roll(pltpu.roll(v, n - k, 0), S5_GROUP * k, 1)
        dn = pltpu.roll(pltpu.roll(v, k, 0), w - S5_GROUP * k, 1)
        rbit = (rows & k) != 0
        gbit = (lanes & (S5_GROUP * k)) != 0
        v = jnp.where(gbit, jnp.where(rbit, v, up), jnp.where(rbit, dn, v))
        k //= 2
    return v


def _s5_kernel(x_ref, t_ref, e_ref, c_ref, dec_ref, h0_ref, *rest, nbk, nc):
    n_state_refs = 6 * S5_GROUPS_IN_FLIGHT
    y_ref, hfin_ref, v0, v1, w0, w1 = rest[-(6 + n_state_refs):-n_state_refs]
    state_refs = rest[-n_state_refs:]
    lw = x_ref.shape[1]
    gpt = lw // S5_GROUP
    nr = nbk * nc
    p2 = state_refs[0].shape[1]
    n_blk = x_ref.shape[0] // S5_TRANSPOSE_ROWS

    def to_groups(i, carry):
        rows = pl.ds(pl.multiple_of(i * S5_TRANSPOSE_ROWS, S5_TRANSPOSE_ROWS), S5_TRANSPOSE_ROWS)
        v = _block_transpose(x_ref[rows, :])
        v0[rows, :] = v[:, :LANES]
        v1[rows, :] = v[:, LANES:]
        return carry

    lax.fori_loop(0, n_blk, to_groups, 0, unroll=S5_TRANSPOSE_UNROLL)

    def group(g, zf, zfs, zb, zbs, hsf, hsb):
        rows_g = pl.ds(g, nr, stride=gpt)
        u = jnp.concatenate([v0[rows_g, :], v1[rows_g, :]], axis=1).astype(BF16)
        z = jnp.dot(u, e_ref[g], preferred_element_type=F32)
        zf[...] = z[:, 0:p2]
        zfs[...] = z[:, p2:2 * p2]
        zb[...] = z[:, 2 * p2:3 * p2]
        zbs[...] = z[:, 3 * p2:4 * p2]
        ar_f, ai_f, ar_b, ai_b = dec_ref[0, g, 0], dec_ref[0, g, 1], dec_ref[1, g, 0], dec_ref[1, g, 1]
        hf, hfs, hb, hbs = h0_ref[g, 0], h0_ref[g, 1], h0_ref[g, 2], h0_ref[g, 3]
        for c in range(nc):
            rf = pl.ds(c, nbk, stride=nc)
            rb = pl.ds(nc - 1 - c, nbk, stride=nc)
            hsf[rf, :] = hf
            hsb[rb, :] = hb
            hf, hfs = ar_f * hf + ai_f * hfs + zf[rf, :], ar_f * hfs - ai_f * hf + zfs[rf, :]
            hb, hbs = ar_b * hb + ai_b * hbs + zb[rb, :], ar_b * hbs - ai_b * hb + zbs[rb, :]
        hfin_ref[g, 0] = hf
        hfin_ref[g, 1] = hb
        hs = jnp.concatenate([hsf[...], hsb[...]], axis=1).astype(BF16)
        y = jnp.dot(u, t_ref[g], preferred_element_type=F32)
        y = y + jnp.dot(hs, c_ref[g], preferred_element_type=F32)
        w0[rows_g, :] = y[:, :LANES]
        w1[rows_g, :] = y[:, LANES:]

    def groups(i, carry):
        for k in range(S5_GROUPS_IN_FLIGHT):
            group(i * S5_GROUPS_IN_FLIGHT + k, *state_refs[6 * k:6 * k + 6])
        return carry

    lax.fori_loop(0, gpt // S5_GROUPS_IN_FLIGHT, groups, 0)

    def to_tokens(i, carry):
        rows = pl.ds(pl.multiple_of(i * S5_TRANSPOSE_ROWS, S5_TRANSPOSE_ROWS), S5_TRANSPOSE_ROWS)
        y_ref[rows, :] = _block_transpose(jnp.concatenate([w0[rows, :], w1[rows, :]], axis=1))
        return carry

    lax.fori_loop(0, n_blk, to_tokens, 0, unroll=S5_TRANSPOSE_UNROLL)


def _s5_scan(proj, row0, h0, mats, n_seq, seq_len, width, y_prev):
    tmat, e_all, c_all, decay = mats
    n = tmat.shape[1]
    p2 = decay.shape[-1]
    lw = S5_LANE_TILE
    gpt = lw // S5_GROUP
    nc = seq_len // S5_CHUNK
    nbk = max(1, S5_ROW_BLOCK // seq_len)
    rb_rows = nbk * seq_len
    n_rb = n_seq // nbk
    nr = nbk * nc
    rb0 = row0 // rb_rows
    assert n_seq % nbk == 0 and row0 % rb_rows == 0 and p2 == LANES and lw == 2 * LANES
    g = tmat.shape[0]
    h0s = jnp.concatenate([h0[..., p2 // 2:], h0[..., :p2 // 2]], axis=-1)
    h04 = jnp.stack([h0[:, 0], h0s[:, 0], h0[:, 1], h0s[:, 1]], axis=1)
    h0p = jnp.transpose(h04.reshape(n_rb, nbk, 4, g, p2), (0, 3, 2, 1, 4))
    wspec = lambda r, c: pl.BlockSpec((gpt, r, c), lambda lt, rb: (lt, 0, 0))
    in_specs = [
        pl.BlockSpec((rb_rows, lw), lambda lt, rb: (rb0 + rb, lt)),
        wspec(n, n), wspec(n, 4 * p2), wspec(2 * p2, n),
        pl.BlockSpec((2, gpt, 2, 1, p2), lambda lt, rb: (0, lt, 0, 0, 0)),
        pl.BlockSpec((None, gpt, 4, nbk, p2), lambda lt, rb: (rb, lt, 0, 0, 0)),
    ]
    args = [proj, tmat, e_all, c_all, decay, h0p]
    aliases = {}
    if y_prev is not None:
        in_specs.append(pl.BlockSpec(memory_space=pl.ANY))
        args.append(y_prev)
        aliases = {6: 0}
    y, hfin = pl.pallas_call(
        functools.partial(_s5_kernel, nbk=nbk, nc=nc),
        out_shape=(jax.ShapeDtypeStruct((proj.shape[0], width), F32),
                   jax.ShapeDtypeStruct((n_rb, g, 2, nbk, p2), F32)),
        grid=(width // lw, n_rb),
        in_specs=in_specs,
        out_specs=(pl.BlockSpec((rb_rows, lw), lambda lt, rb: (rb0 + rb, lt)),
                   pl.BlockSpec((None, gpt, 2, nbk, p2), lambda lt, rb: (rb, lt, 0, 0, 0))),
        scratch_shapes=([pltpu.VMEM((rb_rows, LANES), F32)] * 4
                        + [pltpu.VMEM((nr, p2), F32)] * (6 * S5_GROUPS_IN_FLIGHT)),
        input_output_aliases=aliases,
        compiler_params=_cparams("arbitrary", "arbitrary"),
        name="s5_scan",
    )(*args)
    fin = jnp.transpose(hfin, (0, 3, 2, 1, 4)).reshape(n_seq, 2, g, p2)
    return y, fin


def _s5_glu_kernel(y_ref, u_ref, d_ref, w_ref, b_ref, o_ref, wb_ref):
    @pl.when(pl.program_id(0) == 0)
    def _():
        wb_ref[...] = w_ref[...].astype(BF16)

    z = _gelu(y_ref[...] + d_ref[...] * u_ref[...])
    gate = jnp.dot(z.astype(BF16), wb_ref[...], preferred_element_type=F32) + b_ref[...]
    o_ref[...] = (z * jax.nn.sigmoid(gate)).astype(o_ref.dtype)


def _s5_glu(y, proj, d_row, w_glu, w_layer, b_row):
    t, w = y.shape
    tm = MM_TM
    return pl.pallas_call(
        _s5_glu_kernel,
        out_shape=jax.ShapeDtypeStruct((t, w), BF16),
        grid=(t // tm,),
        in_specs=[
            pl.BlockSpec((tm, w), lambda i: (i, 0)),
            pl.BlockSpec((tm, w), lambda i: (i, 0)),
            pl.BlockSpec((1, w), lambda i: (0, 0)),
            pl.BlockSpec((None, w, w), lambda i: (w_layer, 0, 0)),
            pl.BlockSpec((1, w), lambda i: (0, 0)),
        ],
        out_specs=pl.BlockSpec((tm, w), lambda i: (i, 0)),
        scratch_shapes=[pltpu.VMEM((w, w), BF16)],
        compiler_params=_cparams("arbitrary"),
        name="s5_glu",
    )(y, proj, d_row, w_glu, b_row)


def _lru_kernel(gate_ref, xr_ref, cw_ref, cb_ref, wa_ref, wx_ref, ba_ref, bx_ref, sp_ref, h0_ref, *rest):
    y_ref, hfin_ref, a_f, b_f, a_b, b_b = rest[-6:]
    seq, lw = xr_ref.shape
    x = xr_ref[...]
    rows = lax.broadcasted_iota(jnp.int32, (seq, lw), 0)
    cw = cw_ref[...]
    xc = cw[2:3] * x + cb_ref[...]
    xc = xc + cw[0:1] * jnp.where(rows >= 2, pltpu.roll(x, 2, 0), 0.0)
    xc = xc + cw[1:2] * jnp.where(rows >= 1, pltpu.roll(x, 1, 0), 0.0)
    xc = xc + cw[3:4] * jnp.where(rows < seq - 1, pltpu.roll(x, seq - 1, 0), 0.0)
    for hb in range(lw // LRU_HEAD_BLOCK):
        lanes = slice(hb * LRU_HEAD_BLOCK, (hb + 1) * LRU_HEAD_BLOCK)
        xb = xc[:, lanes]
        xbb = xb.astype(BF16)
        for d, (a_s, b_s) in enumerate(((a_f, b_f), (a_b, b_b))):
            r = jax.nn.sigmoid(jnp.dot(xbb, wa_ref[d, hb], preferred_element_type=F32) + ba_ref[d:d + 1, lanes])
            gi = jax.nn.sigmoid(jnp.dot(xbb, wx_ref[d, hb], preferred_element_type=F32) + bx_ref[d:d + 1, lanes])
            a = jnp.exp(-LRU_C * r * sp_ref[d:d + 1, lanes])
            a_s[:, lanes] = a
            b_s[:, lanes] = jnp.sqrt(1.0 - a * a) * (gi * xb)

    def step(t, carry):
        hf, hb = carry
        rf = pl.ds(t, 1)
        hf = a_f[rf, :] * hf + b_f[rf, :]
        b_f[rf, :] = hf
        rb = pl.ds(seq - 1 - t, 1)
        hb = a_b[rb, :] * hb + b_b[rb, :]
        b_b[rb, :] = hb
        return hf, hb

    hf, hb = lax.fori_loop(0, seq, step, (h0_ref[0:1, :], h0_ref[1:2, :]), unroll=8)
    hfin_ref[0:1, :] = hf
    hfin_ref[1:2, :] = hb
    y_ref[...] = (_gelu(gate_ref[...]) * (b_f[...] + b_b[...])).astype(y_ref.dtype)


def _lru_mixer(proj, row0, n_seq, seq_len, lane_w, h0, conv_w, conv_b, wa_bd, wx_bd, b_a, b_x, sp, y_prev):
    w = conv_w.shape[1]
    nlb = w // lane_w
    rb0 = row0 // seq_len
    hpb = lane_w // LRU_HEAD_BLOCK
    extra_specs, extra_args, aliases = [], [], {}
    if y_prev is not None:
        extra_specs, extra_args, aliases = [pl.BlockSpec(memory_space=pl.ANY)], [y_prev], {10: 0}
    return pl.pallas_call(
        _lru_kernel,
        out_shape=(jax.ShapeDtypeStruct((proj.shape[0], w), BF16), jax.ShapeDtypeStruct((n_seq, 2, w), F32)),
        grid=(n_seq, nlb),
        input_output_aliases=aliases,
        in_specs=[
            pl.BlockSpec((seq_len, lane_w), lambda b, c: (rb0 + b, nlb + c)),
            pl.BlockSpec((seq_len, lane_w), lambda b, c: (rb0 + b, 2 * nlb + c)),
            pl.BlockSpec((conv_w.shape[0], lane_w), lambda b, c: (0, c)),
            pl.BlockSpec((1, lane_w), lambda b, c: (0, c)),
            pl.BlockSpec((2, hpb, LRU_HEAD_BLOCK, LRU_HEAD_BLOCK), lambda b, c: (0, c, 0, 0)),
            pl.BlockSpec((2, hpb, LRU_HEAD_BLOCK, LRU_HEAD_BLOCK), lambda b, c: (0, c, 0, 0)),
            pl.BlockSpec((2, lane_w), lambda b, c: (0, c)),
            pl.BlockSpec((2, lane_w), lambda b, c: (0, c)),
            pl.BlockSpec((2, lane_w), lambda b, c: (0, c)),
            pl.BlockSpec((None, 2, lane_w), lambda b, c: (b, 0, c)),
        ] + extra_specs,
        out_specs=(pl.BlockSpec((seq_len, lane_w), lambda b, c: (rb0 + b, c)),
                   pl.BlockSpec((None, 2, lane_w), lambda b, c: (b, 0, c))),
        scratch_shapes=[pltpu.VMEM((seq_len, lane_w), F32)] * 4,
        compiler_params=_cparams("arbitrary", "arbitrary"),
        name="rglru",
    )(proj, proj, conv_w, conv_b, wa_bd, wx_bd, b_a, b_x, sp, h0, *extra_args)


def _block_diag_heads(w):
    two, h, hd, _ = w.shape
    per = LRU_HEAD_BLOCK // hd
    wb = w.reshape(two, h // per, per, hd, hd)
    eye = jnp.eye(per, dtype=w.dtype)
    bd = jnp.einsum('dbkij,kl->dbkilj', wb, eye)
    return bd.reshape(two, h // per, LRU_HEAD_BLOCK, LRU_HEAD_BLOCK).astype(BF16)


def _softmax_pv(parts):
    m = None
    for s, _ in parts:
        mm = jnp.max(s, axis=-1, keepdims=True)
        m = mm if m is None else jnp.maximum(m, mm)
    acc, den = None, None
    for s, v in parts:
        p = jnp.exp(s - m)
        l = jnp.sum(p, axis=-1, keepdims=True)
        o = jnp.dot(p.astype(BF16), v, preferred_element_type=F32)
        acc = o if acc is None else acc + o
        den = l if den is None else den + l
    return acc / den


def _qk(q, k):
    return lax.dot_general(q, k, (((1,), (1,)), ((), ())), preferred_element_type=F32)


def _ctx_attn_kernel(q_ref, k_ref, v_ref, *rest, n_heads):
    o_ref, ck_ref, cv_ref = rest[-3:]
    dh = q_ref.shape[1] // n_heads
    scale = dh ** -0.5
    ck_ref[...] = k_ref[...]
    cv_ref[...] = v_ref[...]
    for h in range(n_heads):
        lanes = slice(h * dh, (h + 1) * dh)
        q = q_ref[:, lanes].astype(BF16)
        k = k_ref[:, lanes].astype(BF16)
        v = v_ref[:, lanes].astype(BF16)
        o_ref[:, lanes] = _softmax_pv([(_qk(q, k) * scale, v)]).astype(o_ref.dtype)


def _ctx_attention(qkv, n_seq, seq_len, n_heads, total_rows, layer_j, n_attn_layers, caches):
    d = qkv.shape[1] // 3
    cache_shape = jax.ShapeDtypeStruct((n_seq, n_attn_layers, seq_len, d), F32)
    cache_spec = pl.BlockSpec((None, None, seq_len, d), lambda b: (b, layer_j, 0, 0))
    in_specs = [pl.BlockSpec((seq_len, d), lambda b, cb=cb: (b, cb)) for cb in range(3)]
    args = [qkv, qkv, qkv]
    aliases = {}
    if caches is not None:
        in_specs += [pl.BlockSpec(memory_space=pl.ANY)] * 2
        args += list(caches)
        aliases = {3: 1, 4: 2}
    return pl.pallas_call(
        functools.partial(_ctx_attn_kernel, n_heads=n_heads),
        out_shape=(jax.ShapeDtypeStruct((total_rows, d), BF16), cache_shape, cache_shape),
        grid=(n_seq,),
        in_specs=in_specs,
        out_specs=(pl.BlockSpec((seq_len, d), lambda b: (b, 0)), cache_spec, cache_spec),
        input_output_aliases=aliases,
        compiler_params=_cparams("arbitrary"),
        name="ctx_attention",
    )(*args)


def _nbr_row_windows(rows):
    kr = min(WIN_ROWS_MAX, rows)
    starts = np.clip(np.arange(rows) - kr // 2, 0, rows - kr)
    groups, r = [], 0
    while r < rows:
        r1 = r
        while r1 < rows and starts[r1] == starts[r]:
            r1 += 1
        groups.append((r, r1, int(starts[r])))
        r = r1
    return kr, starts, groups


def _nbr_attn_kernel(q_ref, k_ref, v_ref, kc_ref, vc_ref, bias_ref, o_in_ref, o_ref):
    del o_in_ref
    dh = q_ref.shape[1]
    scale = dh ** -0.5
    kr, starts, groups = _nbr_row_windows(q_ref.shape[0] // GRID_W)
    k_all = k_ref[...].astype(BF16)
    v_all = v_ref[...].astype(BF16)
    kc = kc_ref[...].astype(BF16)
    vc = vc_ref[...].astype(BF16)
    for r0, r1, rs in groups:
        q_rows = slice(r0 * GRID_W, r1 * GRID_W)
        k_rows = slice(rs * GRID_W, (rs + kr) * GRID_W)
        q = q_ref[q_rows, :].astype(BF16)
        bias = jnp.concatenate([bias_ref[int(starts[r]) - r + WIN_ROWS_MAX - 1] for r in range(r0, r1)], axis=0)
        s_loc = _qk(q, k_all[k_rows]) * scale + bias
        s_ctx = _qk(q, kc) * scale
        o_ref[q_rows, :] = _softmax_pv([(s_loc, v_all[k_rows]), (s_ctx, vc)]).astype(o_ref.dtype)


def _nbr_attention(qkv, o_all, cache_k, cache_v, layer_j, bias, row0, n_seq, seq_len, n_heads):
    d = qkv.shape[1] // 3
    dh = d // n_heads
    rb0 = row0 // seq_len
    past = cache_k.shape[2]
    ck = cache_k.reshape(cache_k.shape[0], cache_k.shape[1], past, d)
    cv = cache_v.reshape(ck.shape)
    cache_spec = pl.BlockSpec((None, None, past, dh), lambda h, b: (b, layer_j, 0, h))
    return pl.pallas_call(
        _nbr_attn_kernel,
        out_shape=jax.ShapeDtypeStruct(o_all.shape, o_all.dtype),
        grid=(n_heads, n_seq),
        in_specs=[
            pl.BlockSpec((seq_len, dh), lambda h, b: (rb0 + b, h)),
            pl.BlockSpec((seq_len, dh), lambda h, b: (rb0 + b, n_heads + h)),
            pl.BlockSpec((seq_len, dh), lambda h, b: (rb0 + b, 2 * n_heads + h)),
            cache_spec, cache_spec,
            pl.BlockSpec((None,) + bias.shape[1:], lambda h, b: (h, 0, 0, 0)),
            pl.BlockSpec(memory_space=pl.ANY),
        ],
        out_specs=pl.BlockSpec((seq_len, dh), lambda h, b: (rb0 + b, h)),
        input_output_aliases={6: 0},
        compiler_params=_cparams("arbitrary", "arbitrary"),
        name="nbr_attention",
    )(qkv, qkv, qkv, ck, cv, bias, o_all)


def _nbr_bias(rpb, rows):
    kr, _, _ = _nbr_row_windows(rows)
    c_idx = np.arange(GRID_W)
    col_start = np.clip(c_idx - WIN_COLS // 2, 0, GRID_W - WIN_COLS)
    kcol = np.arange(GRID_W)[None, :]
    col_valid = (kcol >= col_start[:, None]) & (kcol < col_start[:, None] + WIN_COLS)
    col_off = np.clip(kcol - c_idx[:, None] + WIN_COLS - 1, 0, 2 * WIN_COLS - 2)
    col_sel = jnp.asarray(np.eye(2 * WIN_COLS - 1, dtype=np.float32)[col_off])
    tab = jnp.einsum('hij,cmj->hcim', rpb.astype(F32), col_sel, precision=lax.Precision.HIGHEST)
    tab = jnp.where(jnp.asarray(col_valid)[None, :, None, :], tab, NEG_INF)
    n_win = 2 * WIN_ROWS_MAX - kr
    wins = [tab[:, :, i0:i0 + kr, :].reshape(tab.shape[0], GRID_W, kr * GRID_W) for i0 in range(n_win)]
    return jnp.stack(wins, axis=1)


def _row_copy(hbm, row, buf, slot, r, sem):
    return pltpu.make_async_copy(hbm.at[pl.ds(row, 1), :], buf.at[slot, pl.ds(r, 1), :], sem.at[slot])


def _moe_gather_kernel(src_ref, nv_ref, h_hbm, o_ref, buf, sem):
    i = pl.program_id(0)
    tm = o_ref.shape[0]
    slot = i % 2

    def issue_tile(tile, s):
        base = tile * tm

        def issue(q, c):
            for par in range(N_DMA_PRIORITIES):
                r = q * N_DMA_PRIORITIES + par
                _row_copy(h_hbm, src_ref[base + r], buf, s, r, sem).start(priority=par)
            return c

        lax.fori_loop(0, tm // N_DMA_PRIORITIES, issue, 0, unroll=4)

    @pl.when(i == 0)
    def _():
        issue_tile(0, 0)

    @pl.when(i + 1 < nv_ref[0])
    def _():
        issue_tile(i + 1, 1 - slot)

    @pl.when(i < nv_ref[0])
    def _():
        pltpu.make_async_copy(h_hbm.at[pl.ds(0, tm), :], buf.at[slot], sem.at[slot]).wait()
        o_ref[...] = buf[slot].astype(o_ref.dtype)

    @pl.when(i >= nv_ref[0])
    def _():
        o_ref[...] = jnp.zeros_like(o_ref)


def _moe_gather(h, src, n_valid, n_rows):
    tm = MOE_TM
    d = h.shape[1]
    return pl.pallas_call(
        _moe_gather_kernel,
        out_shape=jax.ShapeDtypeStruct((n_rows, d), BF16),
        grid_spec=pltpu.PrefetchScalarGridSpec(
            num_scalar_prefetch=2,
            grid=(n_rows // tm,),
            in_specs=[pl.BlockSpec(memory_space=pl.ANY)],
            out_specs=pl.BlockSpec((tm, d), lambda i, src, nv: (i, 0)),
            scratch_shapes=[pltpu.VMEM((2, tm, d), F32), pltpu.SemaphoreType.DMA((2,))],
        ),
        compiler_params=_cparams("arbitrary"),
        name="moe_gather",
    )(src, n_valid, h)


def _new_expert(te_ref, i):
    return (i == 0) | (te_ref[i] != te_ref[jnp.maximum(i - 1, 0)])


def _moe_gu_kernel(te_ref, nv_ref, x_ref, wg_ref, wu_ref, o_ref, wgb, wub):
    i = pl.program_id(1)
    valid = i < nv_ref[0]

    @pl.when(valid & _new_expert(te_ref, i))
    def _():
        wgb[...] = wg_ref[...].astype(BF16)
        wub[...] = wu_ref[...].astype(BF16)

    @pl.when(valid)
    def _():
        x = x_ref[...]
        a = jnp.dot(x, wgb[...], preferred_element_type=F32)
        b = jnp.dot(x, wub[...], preferred_element_type=F32)
        o_ref[...] = (a * jax.nn.sigmoid(a) * b).astype(o_ref.dtype)

    @pl.when(jnp.logical_not(valid))
    def _():
        o_ref[...] = jnp.zeros_like(o_ref)


def _moe_gate_up(xs, w_gu, layer, tile_expert, n_valid):
    p, d = xs.shape
    de = w_gu.shape[3] // 2
    tm, tn = MOE_TM, min(MOE_TN, de)
    nj = de // tn
    row = lambda i, nv: jnp.minimum(i, nv[0] - 1)
    return pl.pallas_call(
        _moe_gu_kernel,
        out_shape=jax.ShapeDtypeStruct((p, de), BF16),
        grid_spec=pltpu.PrefetchScalarGridSpec(
            num_scalar_prefetch=2,
            grid=(nj, p // tm),
            in_specs=[
                pl.BlockSpec((tm, d), lambda j, i, te, nv: (row(i, nv), 0)),
                pl.BlockSpec((None, None, d, tn), lambda j, i, te, nv: (layer, te[i], 0, j)),
                pl.BlockSpec((None, None, d, tn), lambda j, i, te, nv: (layer, te[i], 0, nj + j)),
            ],
            out_specs=pl.BlockSpec((tm, tn), lambda j, i, te, nv: (i, j)),
            scratch_shapes=[pltpu.VMEM((d, tn), BF16), pltpu.VMEM((d, tn), BF16)],
        ),
        compiler_params=_cparams("arbitrary", "arbitrary"),
        name="moe_gate_up",
    )(tile_expert, n_valid, xs, w_gu, w_gu)


def _moe_dn_kernel(te_ref, nv_ref, h_ref, w_ref, o_ref, wb):
    i = pl.program_id(0)
    valid = i < nv_ref[0]

    @pl.when(valid & _new_expert(te_ref, i))
    def _():
        wb[...] = w_ref[...].astype(BF16)

    @pl.when(valid)
    def _():
        o_ref[...] = jnp.dot(h_ref[...], wb[...], preferred_element_type=F32)

    @pl.when(jnp.logical_not(valid))
    def _():
        o_ref[...] = jnp.zeros_like(o_ref)


def _moe_down(hmid, w_dn, layer, tile_expert, n_valid):
    p, de = hmid.shape
    d = w_dn.shape[3]
    tm = MOE_TM
    row = lambda i, nv: jnp.minimum(i, nv[0] - 1)
    return pl.pallas_call(
        _moe_dn_kernel,
        out_shape=jax.ShapeDtypeStruct((p, d), F32),
        grid_spec=pltpu.PrefetchScalarGridSpec(
            num_scalar_prefetch=2,
            grid=(p // tm,),
            in_specs=[
                pl.BlockSpec((tm, de), lambda i, te, nv: (row(i, nv), 0)),
                pl.BlockSpec((None, None, de, d), lambda i, te, nv: (layer, te[i], 0, 0)),
            ],
            out_specs=pl.BlockSpec((tm, d), lambda i, te, nv: (i, 0)),
            scratch_shapes=[pltpu.VMEM((de, d), BF16)],
        ),
        compiler_params=_cparams("arbitrary"),
        name="moe_down",
    )(tile_expert, n_valid, hmid, w_dn)


def _moe_combine_kernel(p1_ref, p2_ref, y_hbm, x_ref, w_ref, g_ref, *rest, with_norm):
    if with_norm:
        ng_ref, nsh_ref, nsc_ref, o_ref, h_ref, buf1, buf2, sem = rest
    else:
        o_ref, buf1, buf2, sem = rest
    i = pl.program_id(0)
    tm = o_ref.shape[0]
    slot = i % 2

    def issue_tile(tile, s):
        base = tile * tm

        def issue(r, c):
            _row_copy(y_hbm, p1_ref[base + r], buf1, s, r, sem).start(priority=0)
            _row_copy(y_hbm, p2_ref[base + r], buf2, s, r, sem).start(priority=1)
            return c

        lax.fori_loop(0, tm, issue, 0, unroll=8)

    @pl.when(i == 0)
    def _():
        issue_tile(0, 0)

    @pl.when(i + 1 < pl.num_programs(0))
    def _():
        issue_tile(i + 1, 1 - slot)

    pltpu.make_async_copy(y_hbm.at[pl.ds(0, tm), :], buf1.at[slot], sem.at[slot]).wait()
    pltpu.make_async_copy(y_hbm.at[pl.ds(0, tm), :], buf2.at[slot], sem.at[slot]).wait()
    w = w_ref[...]
    y = w[:, 0:1] * buf1[slot] + w[:, 1:2] * buf2[slot]
    x_new = x_ref[...] + g_ref[...] * y
    o_ref[...] = x_new
    if with_norm:
        h_ref[...] = (_rms(x_new, ng_ref[...]) * (1.0 + nsc_ref[...]) + nsh_ref[...]).astype(h_ref.dtype)


def _moe_combine(y_sorted, x, wts, p1, p2, mods3, layer, lay, next_norm_g):
    t, d = x.shape
    tm = ROW_TILE
    with_norm = next_norm_g is not None
    mod = lambda lyr, which: pl.BlockSpec(
        (None, 1, d), lambda i, p1, p2: (_mod_index(lyr, which, lay.mod_row(i, tm)), 0, 0))
    row_spec = pl.BlockSpec((tm, d), lambda i, p1, p2: (i, 0))
    in_specs = [pl.BlockSpec(memory_space=pl.ANY), row_spec,
                pl.BlockSpec((tm, wts.shape[1]), lambda i, p1, p2: (i, 0)), mod(layer, 5)]
    args = [p1, p2, y_sorted, x, wts, mods3]
    out_shape = jax.ShapeDtypeStruct((t, d), F32)
    out_specs = row_spec
    if with_norm:
        in_specs += [pl.BlockSpec((1, d), lambda i, p1, p2: (0, 0)), mod(layer + 1, 0), mod(layer + 1, 1)]
        args += [next_norm_g, mods3, mods3]
        out_shape = (out_shape, jax.ShapeDtypeStruct((t, d), BF16))
        out_specs = (row_spec, row_spec)
    return pl.pallas_call(
        functools.partial(_moe_combine_kernel, with_norm=with_norm),
        out_shape=out_shape,
        grid_spec=pltpu.PrefetchScalarGridSpec(
            num_scalar_prefetch=2,
            grid=(t // tm,),
            in_specs=in_specs,
            out_specs=out_specs,
            scratch_shapes=[pltpu.VMEM((2, tm, d), F32), pltpu.VMEM((2, tm, d), F32), pltpu.SemaphoreType.DMA((2,))],
        ),
        compiler_params=_cparams("arbitrary"),
        name="moe_combine",
    )(*args)


def _moe_plan(route, n_experts):
    t = route.shape[1]
    tm = MOE_TM
    eid = route[0:2].astype(jnp.int32).reshape(-1)
    onehot = (eid[:, None] == jnp.arange(n_experts, dtype=jnp.int32)[None, :]).astype(jnp.int32)
    csum = jnp.cumsum(onehot, axis=0)
    rank = jnp.take_along_axis(csum, eid[:, None], axis=1)[:, 0] - 1
    counts = csum[-1]
    padded = ((counts + tm - 1) // tm) * tm
    ends = jnp.cumsum(padded)
    offs = ends - padded
    pos = offs[eid] + rank
    n_rows = 2 * t + n_experts * tm
    tok = jnp.tile(jnp.arange(t, dtype=jnp.int32), 2)
    src = (jnp.arange(n_rows, dtype=jnp.int32) % t).at[pos].set(tok)
    n_tiles = n_rows // tm
    n_valid = (ends[-1] // tm).astype(jnp.int32)
    starts = jnp.arange(n_tiles, dtype=jnp.int32) * tm
    te = jnp.sum((starts[:, None] >= ends[None, :]).astype(jnp.int32), axis=1)
    te_last = jnp.sum((((n_valid - 1) * tm) >= ends).astype(jnp.int32))
    te = jnp.where(starts < ends[-1], te, te_last).astype(jnp.int32)
    wts = jnp.transpose(route[2:4])
    wts = jnp.pad(wts, ((0, 0), (0, 6)))
    return src, te, n_valid.reshape(1), pos[:t], pos[t:], wts, n_rows


def _moe_layer(x, h2, route, mods3, layer, lay, w_gu, w_dn, next_norm_g):
    n_experts = w_gu.shape[1]
    src, te, n_valid, p1, p2, wts, n_rows = _moe_plan(route, n_experts)
    xs = _moe_gather(h2, src, n_valid, n_rows)
    hmid = _moe_gate_up(xs, w_gu, layer, te, n_valid)
    ys = _moe_down(hmid, w_dn, layer, te, n_valid)
    return _moe_combine(ys, x, wts, p1, p2, mods3, layer, lay, next_norm_g)


def kernel(x_prompt, x_sample, state_s5_re, state_s5_im, state_lru, cache_attn_k, cache_attn_v, c, c_ctx, w_ada, b_ada, norm1_g, norm2_g, final_norm_g, w_in_even, w_out_even, s5_lam_re, s5_lam_im, s5_log_dt, s5_b_re, s5_b_im, s5_c_re, s5_c_im, s5_d, s5_w_glu, s5_b_glu, lru_conv_w, lru_conv_b, lru_w_a, lru_b_a, lru_w_x, lru_b_x, lru_lam, w_qkv, w_o, rpb, w_router, b_router, w_gate_up, w_down):
    batch, seq, d = x_prompt.shape
    dec_batch, dec_seq, _ = x_sample.shape
    depth = w_ada.shape[0]
    n_heads = cache_attn_k.shape[3]
    s5_w = s5_d.shape[1]
    lru_w = lru_conv_w.shape[2]
    n_groups, n_state = s5_lam_re.shape[2], s5_lam_re.shape[3]
    assert dec_batch < MOD_ROWS
    n_ctx = batch * seq
    assert n_ctx % MM_TM == 0 and dec_seq % MM_TM == 0 and seq % ROW_TILE == 0 and dec_seq % ROW_TILE == 0
    lay = _Layout(n_ctx, dec_batch, dec_seq)
    t = lay.total

    x = jnp.concatenate([x_prompt.reshape(n_ctx, d), x_sample.reshape(dec_batch * dec_seq, d)], axis=0)
    cvec = jnp.zeros((MOD_ROWS, d), F32).at[:dec_batch].set(c).at[dec_batch].set(c_ctx)
    mods = _ada_project(cvec, w_ada, b_ada)
    mods3 = mods.reshape(depth * MOD_ROWS * N_MOD, 1, d)
    w_router_t = jnp.transpose(w_router)
    b_router_col = b_router.reshape(-1, 1)

    s5_re_list, s5_im_list, lru_list = [], [], []
    kv_caches = None
    h1 = _norm_mod(x, norm1_g[0].reshape(1, d), mods3, 0, 0, lay, BF16)
    for l in range(depth):
        j = l // 2
        if l % 2 == 0:
            proj = _matmul([h1], w_in_even, j, F32)
            mats = _s5_matrices(s5_lam_re[j], s5_lam_im[j], s5_log_dt[j], s5_b_re[j], s5_b_im[j],
                                s5_c_re[j], s5_c_im[j])
            zero_h0 = jnp.zeros((batch, 2, n_groups, 2 * n_state), F32)
            lat_h0 = jnp.concatenate([state_s5_re[:, j], state_s5_im[:, j]], axis=-1).astype(F32)
            y_scan, s5_fin = _s5_scan(proj, 0, zero_h0, mats, batch, seq, s5_w, None)
            y_scan, _ = _s5_scan(proj, n_ctx, lat_h0, mats, dec_batch, dec_seq, s5_w, y_scan)
            s5_re_list.append(s5_fin[..., :n_state])
            s5_im_list.append(s5_fin[..., n_state:])
            y_s5 = _s5_glu(y_scan, proj, s5_d[j].reshape(1, s5_w), s5_w_glu, j, s5_b_glu[j].reshape(1, s5_w))
            wa_bd = _block_diag_heads(lru_w_a[j])
            wx_bd = _block_diag_heads(lru_w_x[j])
            sp = jax.nn.softplus(-lru_lam[j].astype(F32))
            lru_args = (lru_conv_w[j], lru_conv_b[j].reshape(1, lru_w), wa_bd, wx_bd, lru_b_a[j], lru_b_x[j], sp)
            y_lru, lru_fin = _lru_mixer(proj, 0, batch, seq, lru_w, jnp.zeros((batch, 2, lru_w), F32),
                                        *lru_args, None)
            y_lru, _ = _lru_mixer(proj, n_ctx, dec_batch, dec_seq, lru_w // 2, state_lru[:, j].astype(F32),
                                  *lru_args, y_lru)
            lru_list.append(lru_fin)
            x = _matmul([y_s5, y_lru], w_out_even, j, F32, resid=(x, mods3, l, 2, lay))
        else:
            qkv = _matmul([h1], w_qkv, j, F32)
            o_all, *kv_caches = _ctx_attention(qkv, batch, seq, n_heads, t, j, depth // 2, kv_caches)
            bias = _nbr_bias(rpb[j], dec_seq // GRID_W)
            o_all = _nbr_attention(qkv, o_all, cache_attn_k, cache_attn_v, j, bias, n_ctx, dec_batch, dec_seq, n_heads)
            x = _matmul([o_all], w_o, j, F32, resid=(x, mods3, l, 2, lay))
        h2, route = _norm_mod_route(x, norm2_g[l].reshape(1, d), mods3, l, lay, w_router_t, b_router_col)
        if l + 1 < depth:
            x, h1 = _moe_layer(x, h2, route, mods3, l, lay, w_gate_up, w_down, norm1_g[l + 1].reshape(1, d))
        else:
            x = _moe_layer(x, h2, route, mods3, l, lay, w_gate_up, w_down, None)

    g_fin = final_norm_g.reshape(1, d)
    y_prompt = _final_norm(x, g_fin, 0, n_ctx).reshape(batch, seq, d)
    y_sample = _final_norm(x, g_fin, n_ctx, t - n_ctx).reshape(dec_batch, dec_seq, d)
    cache_shape = (batch, depth // 2, seq, n_heads, d // n_heads)
    return (y_prompt, y_sample, jnp.stack(s5_re_list, axis=1), jnp.stack(s5_im_list, axis=1),
            jnp.stack(lru_list, axis=1), kv_caches[0].reshape(cache_shape), kv_caches[1].reshape(cache_shape))
```

```python
import functools
import math

import numpy as np
import jax
import jax.numpy as jnp
from jax import lax
from jax.experimental import pallas as pl
from jax.experimental.pallas import tpu as pltpu

F32 = jnp.float32
BF16 = jnp.bfloat16

LANES = 128
N_DMA_PRIORITIES = 2
NORM_EPS = 1e-6
NEG_INF = -1e30
S5_GROUP = 16
S5_CHUNK = 16
S5_LANE_TILE = 256
S5_ROW_BLOCK = 2048
S5_TRANSPOSE_ROWS = 64
S5_TRANSPOSE_UNROLL = 4
S5_GROUPS_IN_FLIGHT = 4
LRU_C = 8.0
LRU_HEAD_BLOCK = 256
N_EXPERT_GROUPS = 4
WIN_ROWS_MAX = 8
WIN_COLS = 16
GRID_W = 64
N_MOD = 6
MOD_ROWS = 8
VMEM_LIMIT = 52 * 1024 * 1024
ROW_TILE = 256
MM_TM = 1024
MM_TN = 1024
MM_TN_RESID = 1024
MOE_TM = 512
MOE_TN = 512


def _cparams(*sem):
    return pltpu.CompilerParams(dimension_semantics=sem, vmem_limit_bytes=VMEM_LIMIT)


def _sigmoid(x):
    return 0.5 * jnp.tanh(0.5 * x) + 0.5


def _gelu(x):
    return 0.5 * x * (1.0 + jnp.tanh(math.sqrt(2.0 / math.pi) * (x + 0.044715 * (x * x * x))))


class _Layout:
    def __init__(self, n_ctx_rows, dec_batch, dec_seq):
        self.n_ctx = n_ctx_rows
        self.dec_batch = dec_batch
        self.dec_seq = dec_seq
        self.total = n_ctx_rows + dec_batch * dec_seq

    def mod_row(self, i, tm):
        nct = self.n_ctx // tm
        per = self.dec_seq // tm
        return jnp.where(i < nct, self.dec_batch, (i - nct) // per)


def _mod_index(layer, which, row):
    return (layer * MOD_ROWS + row) * N_MOD + which


def _ada_kernel(c_ref, w_ref, b_ref, o_ref):
    c = c_ref[...]
    s = (c * jax.nn.sigmoid(c)).astype(BF16)
    o_ref[...] = jnp.dot(s, w_ref[...].astype(BF16), preferred_element_type=F32) + b_ref[...]


def _ada_project(cvec, w_ada, b_ada):
    depth, d, n = w_ada.shape
    tn = MM_TN
    return pl.pallas_call(
        _ada_kernel,
        out_shape=jax.ShapeDtypeStruct((depth, MOD_ROWS, n), F32),
        grid=(depth, n // tn),
        in_specs=[
            pl.BlockSpec((MOD_ROWS, d), lambda l, j: (0, 0)),
            pl.BlockSpec((None, d, tn), lambda l, j: (l, 0, j)),
            pl.BlockSpec((None, 1, tn), lambda l, j: (l, 0, j)),
        ],
        out_specs=pl.BlockSpec((None, MOD_ROWS, tn), lambda l, j: (l, 0, j)),
        compiler_params=_cparams("arbitrary", "arbitrary"),
        name="ada_project",
    )(cvec, w_ada, b_ada.reshape(depth, 1, n))


def _rms(x, g):
    ms = jnp.mean(x * x, axis=-1, keepdims=True)
    return x * lax.rsqrt(ms + NORM_EPS) * g


def _norm_mod_kernel(x_ref, g_ref, sh_ref, sc_ref, o_ref):
    y = _rms(x_ref[...], g_ref[...])
    o_ref[...] = (y * (1.0 + sc_ref[...]) + sh_ref[...]).astype(o_ref.dtype)


def _norm_mod(x, g_row, mods3, layer, which_shift, lay, out_dtype):
    t, d = x.shape
    tm = ROW_TILE
    row = lambda i: lay.mod_row(i, tm)
    return pl.pallas_call(
        _norm_mod_kernel,
        out_shape=jax.ShapeDtypeStruct((t, d), out_dtype),
        grid=(t // tm,),
        in_specs=[
            pl.BlockSpec((tm, d), lambda i: (i, 0)),
            pl.BlockSpec((1, d), lambda i: (0, 0)),
            pl.BlockSpec((None, 1, d), lambda i: (_mod_index(layer, which_shift, row(i)), 0, 0)),
            pl.BlockSpec((None, 1, d), lambda i: (_mod_index(layer, which_shift + 1, row(i)), 0, 0)),
        ],
        out_specs=pl.BlockSpec((tm, d), lambda i: (i, 0)),
        compiler_params=_cparams("arbitrary"),
        name="norm_mod",
    )(x, g_row, mods3, mods3)


def _top2_of4(a):
    m1 = jnp.maximum(jnp.maximum(a[0], a[1]), jnp.maximum(a[2], a[3]))
    i1 = jnp.where(a[0] == m1, 0, jnp.where(a[1] == m1, 1, jnp.where(a[2] == m1, 2, 3)))
    b = [jnp.where(i1 == k, -jnp.inf, a[k]) for k in range(4)]
    m2 = jnp.maximum(jnp.maximum(b[0], b[1]), jnp.maximum(b[2], b[3]))
    i2 = jnp.where(b[0] == m2, 0, jnp.where(b[1] == m2, 1, jnp.where(b[2] == m2, 2, 3)))
    return m1 + m2, i1, i2


def _norm_mod_route_kernel(x_ref, g_ref, sh_ref, sc_ref, wr_ref, br_ref, o_ref, r_ref):
    y = _rms(x_ref[...], g_ref[...])
    h = y * (1.0 + sc_ref[...]) + sh_ref[...]
    o_ref[...] = h
    logits = lax.dot_general(wr_ref[...], h, (((1,), (1,)), ((), ())),
                             precision=lax.Precision.HIGHEST, preferred_element_type=F32)
    scores = jax.nn.sigmoid(logits)
    sel = scores + br_ref[...]
    n_e = scores.shape[0]
    per = n_e // N_EXPERT_GROUPS
    sel_rows = [sel[e:e + 1, :] for e in range(n_e)]
    score_rows = [scores[e:e + 1, :] for e in range(n_e)]
    gs, i1s, i2s = [], [], []
    for gi in range(N_EXPERT_GROUPS):
        s, i1, i2 = _top2_of4(sel_rows[gi * per:(gi + 1) * per])
        gs.append(s)
        i1s.append(i1)
        i2s.append(i2)
    gmax = jnp.maximum(jnp.maximum(gs[0], gs[1]), jnp.maximum(gs[2], gs[3]))
    gsel = jnp.where(gs[0] == gmax, 0, jnp.where(gs[1] == gmax, 1, jnp.where(gs[2] == gmax, 2, 3)))
    l1 = jnp.where(gsel == 0, i1s[0], jnp.where(gsel == 1, i1s[1], jnp.where(gsel == 2, i1s[2], i1s[3])))
    l2 = jnp.where(gsel == 0, i2s[0], jnp.where(gsel == 1, i2s[1], jnp.where(gsel == 2, i2s[2], i2s[3])))
    e1 = gsel * per + l1
    e2 = gsel * per + l2
    w1 = jnp.zeros_like(gmax)
    w2 = jnp.zeros_like(gmax)
    for e in range(n_e):
        w1 = jnp.where(e1 == e, score_rows[e], w1)
        w2 = jnp.where(e2 == e, score_rows[e], w2)
    wsum = w1 + w2
    zero = jnp.zeros_like(gmax)
    r_ref[...] = jnp.concatenate(
        [e1.astype(F32), e2.astype(F32), w1 / wsum, w2 / wsum, zero, zero, zero, zero], axis=0)


def _norm_mod_route(x, g_row, mods3, layer, lay, w_router_t, b_router_col):
    t, d = x.shape
    tm = ROW_TILE
    n_e = w_router_t.shape[0]
    row = lambda i: lay.mod_row(i, tm)
    return pl.pallas_call(
        _norm_mod_route_kernel,
        out_shape=(jax.ShapeDtypeStruct((t, d), F32), jax.ShapeDtypeStruct((8, t), F32)),
        grid=(t // tm,),
        in_specs=[
            pl.BlockSpec((tm, d), lambda i: (i, 0)),
            pl.BlockSpec((1, d), lambda i: (0, 0)),
            pl.BlockSpec((None, 1, d), lambda i: (_mod_index(layer, 3, row(i)), 0, 0)),
            pl.BlockSpec((None, 1, d), lambda i: (_mod_index(layer, 4, row(i)), 0, 0)),
            pl.BlockSpec((n_e, d), lambda i: (0, 0)),
            pl.BlockSpec((n_e, 1), lambda i: (0, 0)),
        ],
        out_specs=(pl.BlockSpec((tm, d), lambda i: (i, 0)), pl.BlockSpec((8, tm), lambda i: (0, i))),
        compiler_params=_cparams("arbitrary"),
        name="norm_mod_route",
    )(x, g_row, mods3, mods3, w_router_t, b_router_col)


def _final_norm_kernel(x_ref, g_ref, o_ref):
    o_ref[...] = _rms(x_ref[...], g_ref[...])


def _final_norm(x, g_row, row0, n_rows):
    d = x.shape[1]
    tm = ROW_TILE
    return pl.pallas_call(
        _final_norm_kernel,
        out_shape=jax.ShapeDtypeStruct((n_rows, d), F32),
        grid=(n_rows // tm,),
        in_specs=[pl.BlockSpec((tm, d), lambda i: (row0 // tm + i, 0)), pl.BlockSpec((1, d), lambda i: (0, 0))],
        out_specs=pl.BlockSpec((tm, d), lambda i: (i, 0)),
        compiler_params=_cparams("arbitrary"),
        name="final_norm",
    )(x, g_row)


def _mm_kernel(*refs, n_a, resid):
    a_refs = refs[:n_a]
    w_ref = refs[n_a]
    pos = n_a + 1
    if resid:
        x_ref, g_ref = refs[pos], refs[pos + 1]
        pos += 2
    o_ref, wb_ref = refs[pos], refs[pos + 1]

    @pl.when(pl.program_id(1) == 0)
    def _():
        wb_ref[...] = w_ref[...].astype(BF16)

    acc = None
    k0 = 0
    for a_ref in a_refs:
        ka = a_ref.shape[1]
        part = jnp.dot(a_ref[...], wb_ref[k0:k0 + ka, :], preferred_element_type=F32)
        acc = part if acc is None else acc + part
        k0 += ka
    if resid:
        o_ref[...] = x_ref[...] + g_ref[...] * acc
    else:
        o_ref[...] = acc.astype(o_ref.dtype)


def _matmul(a_list, w, w_layer, out_dtype, resid=None):
    m = a_list[0].shape[0]
    _, k, n = w.shape
    tm = MM_TM
    tn = min(MM_TN_RESID if resid is not None else MM_TN, n)
    in_specs = [pl.BlockSpec((tm, a.shape[1]), lambda j, i: (i, 0)) for a in a_list]
    in_specs.append(pl.BlockSpec((None, k, tn), lambda j, i: (w_layer, 0, j)))
    args = list(a_list) + [w]
    if resid is not None:
        x, mods3, layer, which, lay = resid
        in_specs.append(pl.BlockSpec((tm, tn), lambda j, i: (i, j)))
        in_specs.append(pl.BlockSpec(
            (None, 1, tn), lambda j, i: (_mod_index(layer, which, lay.mod_row(i, tm)), 0, j)))
        args += [x, mods3]
    return pl.pallas_call(
        functools.partial(_mm_kernel, n_a=len(a_list), resid=resid is not None),
        out_shape=jax.ShapeDtypeStruct((m, n), out_dtype),
        grid=(n // tn, m // tm),
        in_specs=in_specs,
        out_specs=pl.BlockSpec((tm, tn), lambda j, i: (i, j)),
        scratch_shapes=[pltpu.VMEM((k, tn), BF16)],
        compiler_params=_cparams("arbitrary", "arbitrary"),
        name="matmul_resid" if resid is not None else "matmul",
    )(*args)


def _cmul(ar, ai, br, bi):
    return ar * br - ai * bi, ar * bi + ai * br


def _s5_matrices(lam_re, lam_im, log_dt, b_re, b_im, c_re, c_im):
    s = S5_CHUNK
    f = lambda z: z.astype(F32)
    lam_re, lam_im, log_dt, b_re, b_im, c_re, c_im = map(f, (lam_re, lam_im, log_dt, b_re, b_im, c_re, c_im))
    dt = jnp.exp(log_dt)[..., None]
    kk = jnp.arange(s + 1, dtype=F32)[:, None, None, None]
    mag = jnp.exp(kk * (lam_re * dt)[None])
    ph = kk * (lam_im * dt)[None]
    pw_re, pw_im = mag * jnp.cos(ph), mag * jnp.sin(ph)
    a_re, a_im = pw_re[1], pw_im[1]
    den = lam_re * lam_re + lam_im * lam_im
    q_re, q_im = _cmul(a_re - 1.0, a_im, lam_re / den, -lam_im / den)
    bb_re, bb_im = _cmul(q_re[..., None], q_im[..., None], b_re, b_im)
    cp_re, cp_im = _cmul(c_re[:, :, None], c_im[:, :, None],
                         jnp.moveaxis(pw_re, 0, 2)[:, :, :, None, :], jnp.moveaxis(pw_im, 0, 2)[:, :, :, None, :])
    bt_re = jnp.swapaxes(bb_re, 2, 3)[:, :, None, None]
    bt_im = jnp.swapaxes(bb_im, 2, 3)[:, :, None, None]
    kern = jnp.sum(cp_re[:, :, :, :, None, :] * bt_re - cp_im[:, :, :, :, None, :] * bt_im, axis=-1)
    sp = np.arange(s)[:, None]
    so = np.arange(s)[None, :]
    lag_f = np.clip(so - sp, 0, s)
    lag_b = np.clip(sp - so, 0, s)
    kf = jnp.where(jnp.asarray(so >= sp)[None, :, :, None, None], kern[0][:, lag_f], 0.0)
    kb = jnp.where(jnp.asarray(sp >= so)[None, :, :, None, None], kern[1][:, lag_b], 0.0)
    g = kern.shape[1]
    n = s * S5_GROUP
    tmat = jnp.transpose(kf + kb, (0, 1, 4, 2, 3)).reshape(g, n, n)

    def e_mat(d, powers):
        pr = jnp.moveaxis(pw_re[powers, d], 0, 1)[:, :, :, None]
        pi = jnp.moveaxis(pw_im[powers, d], 0, 1)[:, :, :, None]
        er, ei = _cmul(pr, pi, bb_re[d][:, None], bb_im[d][:, None])
        er = jnp.transpose(er, (0, 1, 3, 2)).reshape(g, n, -1)
        ei = jnp.transpose(ei, (0, 1, 3, 2)).reshape(g, n, -1)
        return jnp.concatenate([er, ei], axis=-1)

    def c_mat(d, powers):
        cr = jnp.transpose(cp_re[d][:, powers], (0, 3, 1, 2)).reshape(g, -1, n)
        ci = jnp.transpose(cp_im[d][:, powers], (0, 3, 1, 2)).reshape(g, -1, n)
        return jnp.concatenate([cr, -ci], axis=1)

    swap = lambda e: jnp.concatenate([e[..., e.shape[-1] // 2:], e[..., :e.shape[-1] // 2]], axis=-1)
    e_f = e_mat(0, np.arange(s - 1, -1, -1))
    e_b = e_mat(1, np.arange(s))
    e_all = jnp.concatenate([e_f, swap(e_f), e_b, swap(e_b)], axis=-1)
    c_all = jnp.concatenate([c_mat(0, np.arange(1, s + 1)), c_mat(1, np.arange(s, 0, -1))], axis=1)
    dec_r = jnp.concatenate([pw_re[s], pw_re[s]], axis=-1)
    dec_i = jnp.concatenate([-pw_im[s], pw_im[s]], axis=-1)
    decay = jnp.stack([dec_r, dec_i], axis=2)[:, :, :, None, :]
    return tmat.astype(BF16), e_all.astype(BF16), c_all.astype(BF16), decay


def _block_transpose(v):
    n, w = v.shape
    rows = lax.broadcasted_iota(jnp.int32, v.shape, 0)
    lanes = lax.broadcasted_iota(jnp.int32, v.shape, 1)
    k = S5_CHUNK // 2
    while k >= 1:
        up = pltpu.roll(pltpu.roll(v, n - k, 0), S5_GROUP * k, 1)
        dn = pltpu.roll(pltpu.roll(v, k, 0), w - S5_GROUP * k, 1)
        rbit = (rows & k) != 0
        gbit = (lanes & (S5_GROUP * k)) != 0
        v = jnp.where(gbit, jnp.where(rbit, v, up), jnp.where(rbit, dn, v))
        k //= 2
    return v


def _s5_kernel(x_ref, t_ref, e_ref, c_ref, dec_ref, h0_ref, *rest, nbk, nc):
    n_state_refs = 6 * S5_GROUPS_IN_FLIGHT
    y_ref, hfin_ref, v0, v1, w0, w1 = rest[-(6 + n_state_refs):-n_state_refs]
    state_refs = rest[-n_state_refs:]
    lw = x_ref.shape[1]
    gpt = lw // S5_GROUP
    nr = nbk * nc
    p2 = state_refs[0].shape[1]
    n_blk = x_ref.shape[0] // S5_TRANSPOSE_ROWS

    def to_groups(i, carry):
        rows = pl.ds(pl.multiple_of(i * S5_TRANSPOSE_ROWS, S5_TRANSPOSE_ROWS), S5_TRANSPOSE_ROWS)
        v = _block_transpose(x_ref[rows, :])
        v0[rows, :] = v[:, :LANES]
        v1[rows, :] = v[:, LANES:]
        return carry

    lax.fori_loop(0, n_blk, to_groups, 0, unroll=S5_TRANSPOSE_UNROLL)

    def group(g, zf, zfs, zb, zbs, hsf, hsb):
        rows_g = pl.ds(g, nr, stride=gpt)
        u = jnp.concatenate([v0[rows_g, :], v1[rows_g, :]], axis=1).astype(BF16)
        z = jnp.dot(u, e_ref[g], preferred_element_type=F32)
        zf[...] = z[:, 0:p2]
        zfs[...] = z[:, p2:2 * p2]
        zb[...] = z[:, 2 * p2:3 * p2]
        zbs[...] = z[:, 3 * p2:4 * p2]
        ar_f, ai_f, ar_b, ai_b = dec_ref[0, g, 0], dec_ref[0, g, 1], dec_ref[1, g, 0], dec_ref[1, g, 1]
        hf, hfs, hb, hbs = h0_ref[g, 0], h0_ref[g, 1], h0_ref[g, 2], h0_ref[g, 3]
        for c in range(nc):
            rf = pl.ds(c, nbk, stride=nc)
            rb = pl.ds(nc - 1 - c, nbk, stride=nc)
            hsf[rf, :] = hf
            hsb[rb, :] = hb
            hf, hfs = ar_f * hf + ai_f * hfs + zf[rf, :], ar_f * hfs - ai_f * hf + zfs[rf, :]
            hb, hbs = ar_b * hb + ai_b * hbs + zb[rb, :], ar_b * hbs - ai_b * hb + zbs[rb, :]
        hfin_ref[g, 0] = hf
        hfin_ref[g, 1] = hb
        hs = jnp.concatenate([hsf[...], hsb[...]], axis=1).astype(BF16)
        y = jnp.dot(u, t_ref[g], preferred_element_type=F32)
        y = y + jnp.dot(hs, c_ref[g], preferred_element_type=F32)
        w0[rows_g, :] = y[:, :LANES]
        w1[rows_g, :] = y[:, LANES:]

    def groups(i, carry):
        for k in range(S5_GROUPS_IN_FLIGHT):
            group(i * S5_GROUPS_IN_FLIGHT + k, *state_refs[6 * k:6 * k + 6])
        return carry

    lax.fori_loop(0, gpt // S5_GROUPS_IN_FLIGHT, groups, 0)

    def to_tokens(i, carry):
        rows = pl.ds(pl.multiple_of(i * S5_TRANSPOSE_ROWS, S5_TRANSPOSE_ROWS), S5_TRANSPOSE_ROWS)
        y_ref[rows, :] = _block_transpose(jnp.concatenate([w0[rows, :], w1[rows, :]], axis=1))
        return carry

    lax.fori_loop(0, n_blk, to_tokens, 0, unroll=S5_TRANSPOSE_UNROLL)


def _s5_scan(proj, row0, h0, mats, n_seq, seq_len, width, y_prev):
    tmat, e_all, c_all, decay = mats
    n = tmat.shape[1]
    p2 = decay.shape[-1]
    lw = S5_LANE_TILE
    gpt = lw // S5_GROUP
    nc = seq_len // S5_CHUNK
    nbk = max(1, S5_ROW_BLOCK // seq_len)
    rb_rows = nbk * seq_len
    n_rb = n_seq // nbk
    nr = nbk * nc
    rb0 = row0 // rb_rows
    assert n_seq % nbk == 0 and row0 % rb_rows == 0 and p2 == LANES and lw == 2 * LANES
    g = tmat.shape[0]
    h0s = jnp.concatenate([h0[..., p2 // 2:], h0[..., :p2 // 2]], axis=-1)
    h04 = jnp.stack([h0[:, 0], h0s[:, 0], h0[:, 1], h0s[:, 1]], axis=1)
    h0p = jnp.transpose(h04.reshape(n_rb, nbk, 4, g, p2), (0, 3, 2, 1, 4))
    wspec = lambda r, c: pl.BlockSpec((gpt, r, c), lambda lt, rb: (lt, 0, 0))
    in_specs = [
        pl.BlockSpec((rb_rows, lw), lambda lt, rb: (rb0 + rb, lt)),
        wspec(n, n), wspec(n, 4 * p2), wspec(2 * p2, n),
        pl.BlockSpec((2, gpt, 2, 1, p2), lambda lt, rb: (0, lt, 0, 0, 0)),
        pl.BlockSpec((None, gpt, 4, nbk, p2), lambda lt, rb: (rb, lt, 0, 0, 0)),
    ]
    args = [proj, tmat, e_all, c_all, decay, h0p]
    aliases = {}
    if y_prev is not None:
        in_specs.append(pl.BlockSpec(memory_space=pl.ANY))
        args.append(y_prev)
        aliases = {6: 0}
    y, hfin = pl.pallas_call(
        functools.partial(_s5_kernel, nbk=nbk, nc=nc),
        out_shape=(jax.ShapeDtypeStruct((proj.shape[0], width), F32),
                   jax.ShapeDtypeStruct((n_rb, g, 2, nbk, p2), F32)),
        grid=(width // lw, n_rb),
        in_specs=in_specs,
        out_specs=(pl.BlockSpec((rb_rows, lw), lambda lt, rb: (rb0 + rb, lt)),
                   pl.BlockSpec((None, gpt, 2, nbk, p2), lambda lt, rb: (rb, lt, 0, 0, 0))),
        scratch_shapes=([pltpu.VMEM((rb_rows, LANES), F32)] * 4
                        + [pltpu.VMEM((nr, p2), F32)] * (6 * S5_GROUPS_IN_FLIGHT)),
        input_output_aliases=aliases,
        compiler_params=_cparams("arbitrary", "arbitrary"),
        name="s5_scan",
    )(*args)
    fin = jnp.transpose(hfin, (0, 3, 2, 1, 4)).reshape(n_seq, 2, g, p2)
    return y, fin


def _s5_glu_kernel(y_ref, u_ref, d_ref, w_ref, b_ref, o_ref, wb_ref):
    @pl.when(pl.program_id(0) == 0)
    def _():
        wb_ref[...] = w_ref[...].astype(BF16)

    z = _gelu(y_ref[...] + d_ref[...] * u_ref[...])
    gate = jnp.dot(z.astype(BF16), wb_ref[...], preferred_element_type=F32) + b_ref[...]
    o_ref[...] = (z * _sigmoid(gate)).astype(o_ref.dtype)


def _s5_glu(y, proj, d_row, w_glu, w_layer, b_row):
    t, w = y.shape
    tm = MM_TM
    return pl.pallas_call(
        _s5_glu_kernel,
        out_shape=jax.ShapeDtypeStruct((t, w), BF16),
        grid=(t // tm,),
        in_specs=[
            pl.BlockSpec((tm, w), lambda i: (i, 0)),
            pl.BlockSpec((tm, w), lambda i: (i, 0)),
            pl.BlockSpec((1, w), lambda i: (0, 0)),
            pl.BlockSpec((None, w, w), lambda i: (w_layer, 0, 0)),
            pl.BlockSpec((1, w), lambda i: (0, 0)),
        ],
        out_specs=pl.BlockSpec((tm, w), lambda i: (i, 0)),
        scratch_shapes=[pltpu.VMEM((w, w), BF16)],
        compiler_params=_cparams("arbitrary"),
        name="s5_glu",
    )(y, proj, d_row, w_glu, b_row)


def _lru_kernel(gate_ref, xr_ref, cw_ref, cb_ref, wa_ref, wx_ref, ba_ref, bx_ref, sp_ref, h0_ref, *rest):
    y_ref, hfin_ref, a_f, b_f, a_b, b_b, h_f, h_b = rest[-8:]
    seq, lw = xr_ref.shape
    x = xr_ref[...]
    rows = lax.broadcasted_iota(jnp.int32, (seq, lw), 0)
    cw = cw_ref[...]
    xc = cw[2:3] * x + cb_ref[...]
    xc = xc + cw[0:1] * jnp.where(rows >= 2, pltpu.roll(x, 2, 0), 0.0)
    xc = xc + cw[1:2] * jnp.where(rows >= 1, pltpu.roll(x, 1, 0), 0.0)
    xc = xc + cw[3:4] * jnp.where(rows < seq - 1, pltpu.roll(x, seq - 1, 0), 0.0)
    for hb in range(lw // LRU_HEAD_BLOCK):
        lanes = slice(hb * LRU_HEAD_BLOCK, (hb + 1) * LRU_HEAD_BLOCK)
        xb = xc[:, lanes]
        xbb = xb.astype(BF16)
        for d, (a_s, b_s) in enumerate(((a_f, b_f), (a_b, b_b))):
            r = _sigmoid(jnp.dot(xbb, wa_ref[d, hb], preferred_element_type=F32) + ba_ref[d:d + 1, lanes])
            gi = _sigmoid(jnp.dot(xbb, wx_ref[d, hb], preferred_element_type=F32) + bx_ref[d:d + 1, lanes])
            a = jnp.exp(-LRU_C * r * sp_ref[d:d + 1, lanes])
            a_s[:, lanes] = a
            b_s[:, lanes] = jnp.sqrt(1.0 - a * a) * (gi * xb)

    def step(t, carry):
        hf, hb = carry
        rf = pl.ds(t, 1)
        hf = a_f[rf, :] * hf + b_f[rf, :]
        h_f[rf, :] = hf
        rb = pl.ds(seq - 1 - t, 1)
        hb = a_b[rb, :] * hb + b_b[rb, :]
        h_b[rb, :] = hb
        return hf, hb

    hf, hb = lax.fori_loop(0, seq, step, (h0_ref[0:1, :], h0_ref[1:2, :]), unroll=8)
    hfin_ref[0:1, :] = hf
    hfin_ref[1:2, :] = hb
    y_ref[...] = (_gelu(gate_ref[...]) * (h_f[...] + h_b[...])).astype(y_ref.dtype)


def _lru_mixer(proj, row0, n_seq, seq_len, lane_w, h0, conv_w, conv_b, wa_bd, wx_bd, b_a, b_x, sp, y_prev):
    w = conv_w.shape[1]
    nlb = w // lane_w
    rb0 = row0 // seq_len
    hpb = lane_w // LRU_HEAD_BLOCK
    extra_specs, extra_args, aliases = [], [], {}
    if y_prev is not None:
        extra_specs, extra_args, aliases = [pl.BlockSpec(memory_space=pl.ANY)], [y_prev], {10: 0}
    return pl.pallas_call(
        _lru_kernel,
        out_shape=(jax.ShapeDtypeStruct((proj.shape[0], w), BF16), jax.ShapeDtypeStruct((n_seq, 2, w), F32)),
        grid=(n_seq, nlb),
        input_output_aliases=aliases,
        in_specs=[
            pl.BlockSpec((seq_len, lane_w), lambda b, c: (rb0 + b, nlb + c)),
            pl.BlockSpec((seq_len, lane_w), lambda b, c: (rb0 + b, 2 * nlb + c)),
            pl.BlockSpec((conv_w.shape[0], lane_w), lambda b, c: (0, c)),
            pl.BlockSpec((1, lane_w), lambda b, c: (0, c)),
            pl.BlockSpec((2, hpb, LRU_HEAD_BLOCK, LRU_HEAD_BLOCK), lambda b, c: (0, c, 0, 0)),
            pl.BlockSpec((2, hpb, LRU_HEAD_BLOCK, LRU_HEAD_BLOCK), lambda b, c: (0, c, 0, 0)),
            pl.BlockSpec((2, lane_w), lambda b, c: (0, c)),
            pl.BlockSpec((2, lane_w), lambda b, c: (0, c)),
            pl.BlockSpec((2, lane_w), lambda b, c: (0, c)),
            pl.BlockSpec((None, 2, lane_w), lambda b, c: (b, 0, c)),
        ] + extra_specs,
        out_specs=(pl.BlockSpec((seq_len, lane_w), lambda b, c: (rb0 + b, c)),
                   pl.BlockSpec((None, 2, lane_w), lambda b, c: (b, 0, c))),
        scratch_shapes=[pltpu.VMEM((seq_len, lane_w), F32)] * 6,
        compiler_params=_cparams("arbitrary", "arbitrary"),
        name="rglru",
    )(proj, proj, conv_w, conv_b, wa_bd, wx_bd, b_a, b_x, sp, h0, *extra_args)


def _block_diag_heads(w):
    two, h, hd, _ = w.shape
    per = LRU_HEAD_BLOCK // hd
    wb = w.reshape(two, h // per, per, hd, hd)
    eye = jnp.eye(per, dtype=w.dtype)
    bd = jnp.einsum('dbkij,kl->dbkilj', wb, eye)
    return bd.reshape(two, h // per, LRU_HEAD_BLOCK, LRU_HEAD_BLOCK).astype(BF16)


def _softmax_pv(parts):
    m = None
    for s, _ in parts:
        mm = jnp.max(s, axis=-1, keepdims=True)
        m = mm if m is None else jnp.maximum(m, mm)
    acc, den = None, None
    for s, v in parts:
        p = jnp.exp(s - m)
        l = jnp.sum(p, axis=-1, keepdims=True)
        o = jnp.dot(p.astype(BF16), v, preferred_element_type=F32)
        acc = o if acc is None else acc + o
        den = l if den is None else den + l
    return acc / den


def _qk(q, k):
    return lax.dot_general(q, k, (((1,), (1,)), ((), ())), preferred_element_type=F32)


def _ctx_attn_kernel(q_ref, k_ref, v_ref, *rest, n_heads):
    o_ref, ck_ref, cv_ref = rest[-3:]
    dh = q_ref.shape[1] // n_heads
    scale = dh ** -0.5
    ck_ref[...] = k_ref[...]
    cv_ref[...] = v_ref[...]
    for h in range(n_heads):
        lanes = slice(h * dh, (h + 1) * dh)
        q = q_ref[:, lanes].astype(BF16)
        k = k_ref[:, lanes].astype(BF16)
        v = v_ref[:, lanes].astype(BF16)
        o_ref[:, lanes] = _softmax_pv([(_qk(q, k) * scale, v)]).astype(o_ref.dtype)


def _ctx_attention(qkv, n_seq, seq_len, n_heads, total_rows, layer_j, n_attn_layers, caches):
    d = qkv.shape[1] // 3
    cache_shape = jax.ShapeDtypeStruct((n_seq, n_attn_layers, seq_len, d), F32)
    cache_spec = pl.BlockSpec((None, None, seq_len, d), lambda b: (b, layer_j, 0, 0))
    in_specs = [pl.BlockSpec((seq_len, d), lambda b, cb=cb: (b, cb)) for cb in range(3)]
    args = [qkv, qkv, qkv]
    aliases = {}
    if caches is not None:
        in_specs += [pl.BlockSpec(memory_space=pl.ANY)] * 2
        args += list(caches)
        aliases = {3: 1, 4: 2}
    return pl.pallas_call(
        functools.partial(_ctx_attn_kernel, n_heads=n_heads),
        out_shape=(jax.ShapeDtypeStruct((total_rows, d), BF16), cache_shape, cache_shape),
        grid=(n_seq,),
        in_specs=in_specs,
        out_specs=(pl.BlockSpec((seq_len, d), lambda b: (b, 0)), cache_spec, cache_spec),
        input_output_aliases=aliases,
        compiler_params=_cparams("arbitrary"),
        name="ctx_attention",
    )(*args)


def _nbr_row_windows(rows):
    kr = min(WIN_ROWS_MAX, rows)
    starts = np.clip(np.arange(rows) - kr // 2, 0, rows - kr)
    groups, r = [], 0
    while r < rows:
        r1 = r
        while r1 < rows and starts[r1] == starts[r]:
            r1 += 1
        groups.append((r, r1, int(starts[r])))
        r = r1
    return kr, starts, groups


def _nbr_attn_kernel(q_ref, k_ref, v_ref, kc_ref, vc_ref, bias_ref, o_in_ref, o_ref):
    del o_in_ref
    dh = q_ref.shape[1]
    scale = dh ** -0.5
    kr, starts, groups = _nbr_row_windows(q_ref.shape[0] // GRID_W)
    k_all = k_ref[...].astype(BF16)
    v_all = v_ref[...].astype(BF16)
    kc = kc_ref[...].astype(BF16)
    vc = vc_ref[...].astype(BF16)
    for r0, r1, rs in groups:
        q_rows = slice(r0 * GRID_W, r1 * GRID_W)
        k_rows = slice(rs * GRID_W, (rs + kr) * GRID_W)
        q = q_ref[q_rows, :].astype(BF16)
        bias = jnp.concatenate([bias_ref[int(starts[r]) - r + WIN_ROWS_MAX - 1] for r in range(r0, r1)], axis=0)
        s_loc = _qk(q, k_all[k_rows]) * scale + bias
        s_ctx = _qk(q, kc) * scale
        o_ref[q_rows, :] = _softmax_pv([(s_loc, v_all[k_rows]), (s_ctx, vc)]).astype(o_ref.dtype)


def _nbr_attention(qkv, o_all, cache_k, cache_v, layer_j, bias, row0, n_seq, seq_len, n_heads):
    d = qkv.shape[1] // 3
    dh = d // n_heads
    rb0 = row0 // seq_len
    past = cache_k.shape[2]
    ck = cache_k.reshape(cache_k.shape[0], cache_k.shape[1], past, d)
    cv = cache_v.reshape(ck.shape)
    cache_spec = pl.BlockSpec((None, None, past, dh), lambda h, b: (b, layer_j, 0, h))
    return pl.pallas_call(
        _nbr_attn_kernel,
        out_shape=jax.ShapeDtypeStruct(o_all.shape, o_all.dtype),
        grid=(n_heads, n_seq),
        in_specs=[
            pl.BlockSpec((seq_len, dh), lambda h, b: (rb0 + b, h)),
            pl.BlockSpec((seq_len, dh), lambda h, b: (rb0 + b, n_heads + h)),
            pl.BlockSpec((seq_len, dh), lambda h, b: (rb0 + b, 2 * n_heads + h)),
            cache_spec, cache_spec,
            pl.BlockSpec((None,) + bias.shape[1:], lambda h, b: (h, 0, 0, 0)),
            pl.BlockSpec(memory_space=pl.ANY),
        ],
        out_specs=pl.BlockSpec((seq_len, dh), lambda h, b: (rb0 + b, h)),
        input_output_aliases={6: 0},
        compiler_params=_cparams("arbitrary", "arbitrary"),
        name="nbr_attention",
    )(qkv, qkv, qkv, ck, cv, bias, o_all)


def _nbr_bias(rpb, rows):
    kr, _, _ = _nbr_row_windows(rows)
    c_idx = np.arange(GRID_W)
    col_start = np.clip(c_idx - WIN_COLS // 2, 0, GRID_W - WIN_COLS)
    kcol = np.arange(GRID_W)[None, :]
    col_valid = (kcol >= col_start[:, None]) & (kcol < col_start[:, None] + WIN_COLS)
    col_off = np.clip(kcol - c_idx[:, None] + WIN_COLS - 1, 0, 2 * WIN_COLS - 2)
    col_sel = jnp.asarray(np.eye(2 * WIN_COLS - 1, dtype=np.float32)[col_off])
    tab = jnp.einsum('hij,cmj->hcim', rpb.astype(F32), col_sel, precision=lax.Precision.HIGHEST)
    tab = jnp.where(jnp.asarray(col_valid)[None, :, None, :], tab, NEG_INF)
    n_win = 2 * WIN_ROWS_MAX - kr
    wins = [tab[:, :, i0:i0 + kr, :].reshape(tab.shape[0], GRID_W, kr * GRID_W) for i0 in range(n_win)]
    return jnp.stack(wins, axis=1)


def _row_copy(hbm, row, buf, slot, r, sem):
    return pltpu.make_async_copy(hbm.at[pl.ds(row, 1), :], buf.at[slot, pl.ds(r, 1), :], sem.at[slot])


def _moe_gather_kernel(src_ref, nv_ref, h_hbm, o_ref, buf, sem):
    i = pl.program_id(0)
    tm = o_ref.shape[0]
    slot = i % 2

    def issue_tile(tile, s):
        base = tile * tm

        def issue(q, c):
            for par in range(N_DMA_PRIORITIES):
                r = q * N_DMA_PRIORITIES + par
                _row_copy(h_hbm, src_ref[base + r], buf, s, r, sem).start(priority=par)
            return c

        lax.fori_loop(0, tm // N_DMA_PRIORITIES, issue, 0, unroll=4)

    @pl.when(i == 0)
    def _():
        issue_tile(0, 0)

    @pl.when(i + 1 < nv_ref[0])
    def _():
        issue_tile(i + 1, 1 - slot)

    @pl.when(i < nv_ref[0])
    def _():
        pltpu.make_async_copy(h_hbm.at[pl.ds(0, tm), :], buf.at[slot], sem.at[slot]).wait()
        o_ref[...] = buf[slot].astype(o_ref.dtype)

    @pl.when(i >= nv_ref[0])
    def _():
        o_ref[...] = jnp.zeros_like(o_ref)


def _moe_gather(h, src, n_valid, n_rows):
    tm = MOE_TM
    d = h.shape[1]
    return pl.pallas_call(
        _moe_gather_kernel,
        out_shape=jax.ShapeDtypeStruct((n_rows, d), BF16),
        grid_spec=pltpu.PrefetchScalarGridSpec(
            num_scalar_prefetch=2,
            grid=(n_rows // tm,),
            in_specs=[pl.BlockSpec(memory_space=pl.ANY)],
            out_specs=pl.BlockSpec((tm, d), lambda i, src, nv: (i, 0)),
            scratch_shapes=[pltpu.VMEM((2, tm, d), F32), pltpu.SemaphoreType.DMA((2,))],
        ),
        compiler_params=_cparams("arbitrary"),
        name="moe_gather",
    )(src, n_valid, h)


def _new_expert(te_ref, i):
    return (i == 0) | (te_ref[i] != te_ref[jnp.maximum(i - 1, 0)])


def _moe_gu_kernel(te_ref, nv_ref, x_ref, wg_ref, wu_ref, o_ref, wgb, wub):
    i = pl.program_id(1)
    valid = i < nv_ref[0]

    @pl.when(valid & _new_expert(te_ref, i))
    def _():
        wgb[...] = wg_ref[...].astype(BF16)
        wub[...] = wu_ref[...].astype(BF16)

    @pl.when(valid)
    def _():
        x = x_ref[...]
        a = jnp.dot(x, wgb[...], preferred_element_type=F32)
        b = jnp.dot(x, wub[...], preferred_element_type=F32)
        o_ref[...] = (a * jax.nn.sigmoid(a) * b).astype(o_ref.dtype)

    @pl.when(jnp.logical_not(valid))
    def _():
        o_ref[...] = jnp.zeros_like(o_ref)


def _moe_gate_up(xs, w_gu, layer, tile_expert, n_valid):
    p, d = xs.shape
    de = w_gu.shape[3] // 2
    tm, tn = MOE_TM, min(MOE_TN, de)
    nj = de // tn
    row = lambda i, nv: jnp.minimum(i, nv[0] - 1)
    return pl.pallas_call(
        _moe_gu_kernel,
        out_shape=jax.ShapeDtypeStruct((p, de), BF16),
        grid_spec=pltpu.PrefetchScalarGridSpec(
            num_scalar_prefetch=2,
            grid=(nj, p // tm),
            in_specs=[
                pl.BlockSpec((tm, d), lambda j, i, te, nv: (row(i, nv), 0)),
                pl.BlockSpec((None, None, d, tn), lambda j, i, te, nv: (layer, te[i], 0, j)),
                pl.BlockSpec((None, None, d, tn), lambda j, i, te, nv: (layer, te[i], 0, nj + j)),
            ],
            out_specs=pl.BlockSpec((tm, tn), lambda j, i, te, nv: (i, j)),
            scratch_shapes=[pltpu.VMEM((d, tn), BF16), pltpu.VMEM((d, tn), BF16)],
        ),
        compiler_params=_cparams("arbitrary", "arbitrary"),
        name="moe_gate_up",
    )(tile_expert, n_valid, xs, w_gu, w_gu)


def _moe_dn_kernel(te_ref, nv_ref, h_ref, w_ref, o_ref, wb):
    i = pl.program_id(0)
    valid = i < nv_ref[0]

    @pl.when(valid & _new_expert(te_ref, i))
    def _():
        wb[...] = w_ref[...].astype(BF16)

    @pl.when(valid)
    def _():
        o_ref[...] = jnp.dot(h_ref[...], wb[...], preferred_element_type=F32)

    @pl.when(jnp.logical_not(valid))
    def _():
        o_ref[...] = jnp.zeros_like(o_ref)


def _moe_down(hmid, w_dn, layer, tile_expert, n_valid):
    p, de = hmid.shape
    d = w_dn.shape[3]
    tm = MOE_TM
    row = lambda i, nv: jnp.minimum(i, nv[0] - 1)
    return pl.pallas_call(
        _moe_dn_kernel,
        out_shape=jax.ShapeDtypeStruct((p, d), F32),
        grid_spec=pltpu.PrefetchScalarGridSpec(
            num_scalar_prefetch=2,
            grid=(p // tm,),
            in_specs=[
                pl.BlockSpec((tm, de), lambda i, te, nv: (row(i, nv), 0)),
                pl.BlockSpec((None, None, de, d), lambda i, te, nv: (layer, te[i], 0, 0)),
            ],
            out_specs=pl.BlockSpec((tm, d), lambda i, te, nv: (i, 0)),
            scratch_shapes=[pltpu.VMEM((de, d), BF16)],
        ),
        compiler_params=_cparams("arbitrary"),
        name="moe_down",
    )(tile_expert, n_valid, hmid, w_dn)


def _moe_combine_kernel(p1_ref, p2_ref, y_hbm, x_ref, w_ref, g_ref, *rest, with_norm):
    if with_norm:
        ng_ref, nsh_ref, nsc_ref, o_ref, h_ref, buf1, buf2, sem = rest
    else:
        o_ref, buf1, buf2, sem = rest
    i = pl.program_id(0)
    tm = o_ref.shape[0]
    slot = i % 2

    def issue_tile(tile, s):
        base = tile * tm

        def issue(r, c):
            _row_copy(y_hbm, p1_ref[base + r], buf1, s, r, sem).start(priority=0)
            _row_copy(y_hbm, p2_ref[base + r], buf2, s, r, sem).start(priority=1)
            return c

        lax.fori_loop(0, tm, issue, 0, unroll=8)

    @pl.when(i == 0)
    def _():
        issue_tile(0, 0)

    @pl.when(i + 1 < pl.num_programs(0))
    def _():
        issue_tile(i + 1, 1 - slot)

    pltpu.make_async_copy(y_hbm.at[pl.ds(0, tm), :], buf1.at[slot], sem.at[slot]).wait()
    pltpu.make_async_copy(y_hbm.at[pl.ds(0, tm), :], buf2.at[slot], sem.at[slot]).wait()
    w = w_ref[...]
    y = w[:, 0:1] * buf1[slot] + w[:, 1:2] * buf2[slot]
    x_new = x_ref[...] + g_ref[...] * y
    o_ref[...] = x_new
    if with_norm:
        h_ref[...] = (_rms(x_new, ng_ref[...]) * (1.0 + nsc_ref[...]) + nsh_ref[...]).astype(h_ref.dtype)


def _moe_combine(y_sorted, x, wts, p1, p2, mods3, layer, lay, next_norm_g):
    t, d = x.shape
    tm = ROW_TILE
    with_norm = next_norm_g is not None
    mod = lambda lyr, which: pl.BlockSpec(
        (None, 1, d), lambda i, p1, p2: (_mod_index(lyr, which, lay.mod_row(i, tm)), 0, 0))
    row_spec = pl.BlockSpec((tm, d), lambda i, p1, p2: (i, 0))
    in_specs = [pl.BlockSpec(memory_space=pl.ANY), row_spec,
                pl.BlockSpec((tm, wts.shape[1]), lambda i, p1, p2: (i, 0)), mod(layer, 5)]
    args = [p1, p2, y_sorted, x, wts, mods3]
    out_shape = jax.ShapeDtypeStruct((t, d), F32)
    out_specs = row_spec
    if with_norm:
        in_specs += [pl.BlockSpec((1, d), lambda i, p1, p2: (0, 0)), mod(layer + 1, 0), mod(layer + 1, 1)]
        args += [next_norm_g, mods3, mods3]
        out_shape = (out_shape, jax.ShapeDtypeStruct((t, d), BF16))
        out_specs = (row_spec, row_spec)
    return pl.pallas_call(
        functools.partial(_moe_combine_kernel, with_norm=with_norm),
        out_shape=out_shape,
        grid_spec=pltpu.PrefetchScalarGridSpec(
            num_scalar_prefetch=2,
            grid=(t // tm,),
            in_specs=in_specs,
            out_specs=out_specs,
            scratch_shapes=[pltpu.VMEM((2, tm, d), F32), pltpu.VMEM((2, tm, d), F32), pltpu.SemaphoreType.DMA((2,))],
        ),
        compiler_params=_cparams("arbitrary"),
        name="moe_combine",
    )(*args)


def _moe_plan(route, n_experts):
    t = route.shape[1]
    tm = MOE_TM
    eid = route[0:2].astype(jnp.int32).reshape(-1)
    onehot = (eid[:, None] == jnp.arange(n_experts, dtype=jnp.int32)[None, :]).astype(jnp.int32)
    csum = jnp.cumsum(onehot, axis=0)
    rank = jnp.take_along_axis(csum, eid[:, None], axis=1)[:, 0] - 1
    counts = csum[-1]
    padded = ((counts + tm - 1) // tm) * tm
    ends = jnp.cumsum(padded)
    offs = ends - padded
    pos = offs[eid] + rank
    n_rows = 2 * t + n_experts * tm
    tok = jnp.tile(jnp.arange(t, dtype=jnp.int32), 2)
    src = (jnp.arange(n_rows, dtype=jnp.int32) % t).at[pos].set(tok)
    n_tiles = n_rows // tm
    n_valid = (ends[-1] // tm).astype(jnp.int32)
    starts = jnp.arange(n_tiles, dtype=jnp.int32) * tm
    te = jnp.sum((starts[:, None] >= ends[None, :]).astype(jnp.int32), axis=1)
    te_last = jnp.sum((((n_valid - 1) * tm) >= ends).astype(jnp.int32))
    te = jnp.where(starts < ends[-1], te, te_last).astype(jnp.int32)
    wts = jnp.transpose(route[2:4])
    wts = jnp.pad(wts, ((0, 0), (0, 6)))
    return src, te, n_valid.reshape(1), pos[:t], pos[t:], wts, n_rows


def _moe_layer(x, h2, route, mods3, layer, lay, w_gu, w_dn, next_norm_g):
    n_experts = w_gu.shape[1]
    src, te, n_valid, p1, p2, wts, n_rows = _moe_plan(route, n_experts)
    xs = _moe_gather(h2, src, n_valid, n_rows)
    hmid = _moe_gate_up(xs, w_gu, layer, te, n_valid)
    ys = _moe_down(hmid, w_dn, layer, te, n_valid)
    return _moe_combine(ys, x, wts, p1, p2, mods3, layer, lay, next_norm_g)


def kernel(x_prompt, x_sample, state_s5_re, state_s5_im, state_lru, cache_attn_k, cache_attn_v, c, c_ctx, w_ada, b_ada, norm1_g, norm2_g, final_norm_g, w_in_even, w_out_even, s5_lam_re, s5_lam_im, s5_log_dt, s5_b_re, s5_b_im, s5_c_re, s5_c_im, s5_d, s5_w_glu, s5_b_glu, lru_conv_w, lru_conv_b, lru_w_a, lru_b_a, lru_w_x, lru_b_x, lru_lam, w_qkv, w_o, rpb, w_router, b_router, w_gate_up, w_down):
    batch, seq, d = x_prompt.shape
    dec_batch, dec_seq, _ = x_sample.shape
    depth = w_ada.shape[0]
    n_heads = cache_attn_k.shape[3]
    s5_w = s5_d.shape[1]
    lru_w = lru_conv_w.shape[2]
    n_groups, n_state = s5_lam_re.shape[2], s5_lam_re.shape[3]
    assert dec_batch < MOD_ROWS
    n_ctx = batch * seq
    assert n_ctx % MM_TM == 0 and dec_seq % MM_TM == 0 and seq % ROW_TILE == 0 and dec_seq % ROW_TILE == 0
    lay = _Layout(n_ctx, dec_batch, dec_seq)
    t = lay.total

    x = jnp.concatenate([x_prompt.reshape(n_ctx, d), x_sample.reshape(dec_batch * dec_seq, d)], axis=0)
    cvec = jnp.zeros((MOD_ROWS, d), F32).at[:dec_batch].set(c).at[dec_batch].set(c_ctx)
    mods = _ada_project(cvec, w_ada, b_ada)
    mods3 = mods.reshape(depth * MOD_ROWS * N_MOD, 1, d)
    w_router_t = jnp.transpose(w_router)
    b_router_col = b_router.reshape(-1, 1)

    s5_re_list, s5_im_list, lru_list = [], [], []
    kv_caches = None
    h1 = _norm_mod(x, norm1_g[0].reshape(1, d), mods3, 0, 0, lay, BF16)
    for l in range(depth):
        j = l // 2
        if l % 2 == 0:
            proj = _matmul([h1], w_in_even, j, F32)
            mats = _s5_matrices(s5_lam_re[j], s5_lam_im[j], s5_log_dt[j], s5_b_re[j], s5_b_im[j],
                                s5_c_re[j], s5_c_im[j])
            zero_h0 = jnp.zeros((batch, 2, n_groups, 2 * n_state), F32)
            lat_h0 = jnp.concatenate([state_s5_re[:, j], state_s5_im[:, j]], axis=-1).astype(F32)
            y_scan, s5_fin = _s5_scan(proj, 0, zero_h0, mats, batch, seq, s5_w, None)
            y_scan, _ = _s5_scan(proj, n_ctx, lat_h0, mats, dec_batch, dec_seq, s5_w, y_scan)
            s5_re_list.append(s5_fin[..., :n_state])
            s5_im_list.append(s5_fin[..., n_state:])
            y_s5 = _s5_glu(y_scan, proj, s5_d[j].reshape(1, s5_w), s5_w_glu, j, s5_b_glu[j].reshape(1, s5_w))
            wa_bd = _block_diag_heads(lru_w_a[j])
            wx_bd = _block_diag_heads(lru_w_x[j])
            sp = jax.nn.softplus(-lru_lam[j].astype(F32))
            lru_args = (lru_conv_w[j], lru_conv_b[j].reshape(1, lru_w), wa_bd, wx_bd, lru_b_a[j], lru_b_x[j], sp)
            y_lru, lru_fin = _lru_mixer(proj, 0, batch, seq, lru_w, jnp.zeros((batch, 2, lru_w), F32),
                                        *lru_args, None)
            y_lru, _ = _lru_mixer(proj, n_ctx, dec_batch, dec_seq, lru_w // 2, state_lru[:, j].astype(F32),
                                  *lru_args, y_lru)
            lru_list.append(lru_fin)
            x = _matmul([y_s5, y_lru], w_out_even, j, F32, resid=(x, mods3, l, 2, lay))
        else:
            qkv = _matmul([h1], w_qkv, j, F32)
            o_all, *kv_caches = _ctx_attention(qkv, batch, seq, n_heads, t, j, depth // 2, kv_caches)
            bias = _nbr_bias(rpb[j], dec_seq // GRID_W)
            o_all = _nbr_attention(qkv, o_all, cache_attn_k, cache_attn_v, j, bias, n_ctx, dec_batch, dec_seq, n_heads)
            x = _matmul([o_all], w_o, j, F32, resid=(x, mods3, l, 2, lay))
        h2, route = _norm_mod_route(x, norm2_g[l].reshape(1, d), mods3, l, lay, w_router_t, b_router_col)
        if l + 1 < depth:
            x, h1 = _moe_layer(x, h2, route, mods3, l, lay, w_gate_up, w_down, norm1_g[l + 1].reshape(1, d))
        else:
            x = _moe_layer(x, h2, route, mods3, l, lay, w_gate_up, w_down, None)

    g_fin = final_norm_g.reshape(1, d)
    y_prompt = _final_norm(x, g_fin, 0, n_ctx).reshape(batch, seq, d)
    y_sample = _final_norm(x, g_fin, n_ctx, t - n_ctx).reshape(dec_batch, dec_seq, d)
    cache_shape = (batch, depth // 2, seq, n_heads, d // n_heads)
    return (y_prompt, y_sample, jnp.stack(s5_re_list, axis=1), jnp.stack(s5_im_list, axis=1),
            jnp.stack(lru_list, axis=1), kv_caches[0].reshape(cache_shape), kv_caches[1].reshape(cache_shape))
```

```python
import functools
import math

import numpy as np
import jax
import jax.numpy as jnp
from jax import lax
from jax.experimental import pallas as pl
from jax.experimental.pallas import tpu as pltpu

F32 = jnp.float32
BF16 = jnp.bfloat16

LANES = 128
N_DMA_PRIORITIES = 2
NORM_EPS = 1e-6
NEG_INF = -1e30
S5_GROUP = 16
S5_CHUNK = 16
S5_LANE_TILE = 256
S5_ROW_BLOCK = 2048
S5_TRANSPOSE_ROWS = 64
S5_TRANSPOSE_UNROLL = 4
S5_GROUPS_IN_FLIGHT = 4
LRU_C = 8.0
LRU_HEAD_BLOCK = 256
N_EXPERT_GROUPS = 4
WIN_ROWS_MAX = 8
WIN_COLS = 16
GRID_W = 64
N_MOD = 6
MOD_ROWS = 8
VMEM_LIMIT = 52 * 1024 * 1024
ROW_TILE = 256
MM_TM = 1024
MM_TN = 1024
MM_TN_RESID = 1024
MOE_TM = 512
MOE_TN = 512


def _cparams(*sem):
    return pltpu.CompilerParams(dimension_semantics=sem, vmem_limit_bytes=VMEM_LIMIT)


def _sigmoid(x):
    return 0.5 * jnp.tanh(0.5 * x) + 0.5


def _gelu(x):
    return 0.5 * x * (1.0 + jnp.tanh(math.sqrt(2.0 / math.pi) * (x + 0.044715 * (x * x * x))))


class _Layout:
    def __init__(self, n_ctx_rows, dec_batch, dec_seq):
        self.n_ctx = n_ctx_rows
        self.dec_batch = dec_batch
        self.dec_seq = dec_seq
        self.total = n_ctx_rows + dec_batch * dec_seq

    def mod_row(self, i, tm):
        nct = self.n_ctx // tm
        per = self.dec_seq // tm
        return jnp.where(i < nct, self.dec_batch, (i - nct) // per)


def _mod_index(layer, which, row):
    return (layer * MOD_ROWS + row) * N_MOD + which


def _ada_kernel(c_ref, w_ref, b_ref, o_ref):
    c = c_ref[...]
    s = (c * jax.nn.sigmoid(c)).astype(BF16)
    o_ref[...] = jnp.dot(s, w_ref[...].astype(BF16), preferred_element_type=F32) + b_ref[...]


def _ada_project(cvec, w_ada, b_ada):
    depth, d, n = w_ada.shape
    tn = MM_TN
    return pl.pallas_call(
        _ada_kernel,
        out_shape=jax.ShapeDtypeStruct((depth, MOD_ROWS, n), F32),
        grid=(depth, n // tn),
        in_specs=[
            pl.BlockSpec((MOD_ROWS, d), lambda l, j: (0, 0)),
            pl.BlockSpec((None, d, tn), lambda l, j: (l, 0, j)),
            pl.BlockSpec((None, 1, tn), lambda l, j: (l, 0, j)),
        ],
        out_specs=pl.BlockSpec((None, MOD_ROWS, tn), lambda l, j: (l, 0, j)),
        compiler_params=_cparams("arbitrary", "arbitrary"),
        name="ada_project",
    )(cvec, w_ada, b_ada.reshape(depth, 1, n))


def _rms(x, g):
    ms = jnp.mean(x * x, axis=-1, keepdims=True)
    return x * lax.rsqrt(ms + NORM_EPS) * g


def _norm_mod_kernel(x_ref, g_ref, sh_ref, sc_ref, o_ref):
    y = _rms(x_ref[...], g_ref[...])
    o_ref[...] = (y * (1.0 + sc_ref[...]) + sh_ref[...]).astype(o_ref.dtype)


def _norm_mod(x, g_row, mods3, layer, which_shift, lay, out_dtype):
    t, d = x.shape
    tm = ROW_TILE
    row = lambda i: lay.mod_row(i, tm)
    return pl.pallas_call(
        _norm_mod_kernel,
        out_shape=jax.ShapeDtypeStruct((t, d), out_dtype),
        grid=(t // tm,),
        in_specs=[
            pl.BlockSpec((tm, d), lambda i: (i, 0)),
            pl.BlockSpec((1, d), lambda i: (0, 0)),
            pl.BlockSpec((None, 1, d), lambda i: (_mod_index(layer, which_shift, row(i)), 0, 0)),
            pl.BlockSpec((None, 1, d), lambda i: (_mod_index(layer, which_shift + 1, row(i)), 0, 0)),
        ],
        out_specs=pl.BlockSpec((tm, d), lambda i: (i, 0)),
        compiler_params=_cparams("arbitrary"),
        name="norm_mod",
    )(x, g_row, mods3, mods3)


def _top2_of4(a):
    m1 = jnp.maximum(jnp.maximum(a[0], a[1]), jnp.maximum(a[2], a[3]))
    i1 = jnp.where(a[0] == m1, 0, jnp.where(a[1] == m1, 1, jnp.where(a[2] == m1, 2, 3)))
    b = [jnp.where(i1 == k, -jnp.inf, a[k]) for k in range(4)]
    m2 = jnp.maximum(jnp.maximum(b[0], b[1]), jnp.maximum(b[2], b[3]))
    i2 = jnp.where(b[0] == m2, 0, jnp.where(b[1] == m2, 1, jnp.where(b[2] == m2, 2, 3)))
    return m1 + m2, i1, i2


def _norm_mod_route_kernel(x_ref, g_ref, sh_ref, sc_ref, wr_ref, br_ref, o_ref, r_ref):
    y = _rms(x_ref[...], g_ref[...])
    h = y * (1.0 + sc_ref[...]) + sh_ref[...]
    o_ref[...] = h
    logits = lax.dot_general(wr_ref[...], h, (((1,), (1,)), ((), ())),
                             precision=lax.Precision.HIGHEST, preferred_element_type=F32)
    scores = jax.nn.sigmoid(logits)
    sel = scores + br_ref[...]
    n_e = scores.shape[0]
    per = n_e // N_EXPERT_GROUPS
    sel_rows = [sel[e:e + 1, :] for e in range(n_e)]
    score_rows = [scores[e:e + 1, :] for e in range(n_e)]
    gs, i1s, i2s = [], [], []
    for gi in range(N_EXPERT_GROUPS):
        s, i1, i2 = _top2_of4(sel_rows[gi * per:(gi + 1) * per])
        gs.append(s)
        i1s.append(i1)
        i2s.append(i2)
    gmax = jnp.maximum(jnp.maximum(gs[0], gs[1]), jnp.maximum(gs[2], gs[3]))
    gsel = jnp.where(gs[0] == gmax, 0, jnp.where(gs[1] == gmax, 1, jnp.where(gs[2] == gmax, 2, 3)))
    l1 = jnp.where(gsel == 0, i1s[0], jnp.where(gsel == 1, i1s[1], jnp.where(gsel == 2, i1s[2], i1s[3])))
    l2 = jnp.where(gsel == 0, i2s[0], jnp.where(gsel == 1, i2s[1], jnp.where(gsel == 2, i2s[2], i2s[3])))
    e1 = gsel * per + l1
    e2 = gsel * per + l2
    w1 = jnp.zeros_like(gmax)
    w2 = jnp.zeros_like(gmax)
    for e in range(n_e):
        w1 = jnp.where(e1 == e, score_rows[e], w1)
        w2 = jnp.where(e2 == e, score_rows[e], w2)
    wsum = w1 + w2
    zero = jnp.zeros_like(gmax)
    r_ref[...] = jnp.concatenate(
        [e1.astype(F32), e2.astype(F32), w1 / wsum, w2 / wsum, zero, zero, zero, zero], axis=0)


def _norm_mod_route(x, g_row, mods3, layer, lay, w_router_t, b_router_col):
    t, d = x.shape
    tm = ROW_TILE
    n_e = w_router_t.shape[0]
    row = lambda i: lay.mod_row(i, tm)
    return pl.pallas_call(
        _norm_mod_route_kernel,
        out_shape=(jax.ShapeDtypeStruct((t, d), F32), jax.ShapeDtypeStruct((8, t), F32)),
        grid=(t // tm,),
        in_specs=[
            pl.BlockSpec((tm, d), lambda i: (i, 0)),
            pl.BlockSpec((1, d), lambda i: (0, 0)),
            pl.BlockSpec((None, 1, d), lambda i: (_mod_index(layer, 3, row(i)), 0, 0)),
            pl.BlockSpec((None, 1, d), lambda i: (_mod_index(layer, 4, row(i)), 0, 0)),
            pl.BlockSpec((n_e, d), lambda i: (0, 0)),
            pl.BlockSpec((n_e, 1), lambda i: (0, 0)),
        ],
        out_specs=(pl.BlockSpec((tm, d), lambda i: (i, 0)), pl.BlockSpec((8, tm), lambda i: (0, i))),
        compiler_params=_cparams("arbitrary"),
        name="norm_mod_route",
    )(x, g_row, mods3, mods3, w_router_t, b_router_col)


def _final_norm_kernel(x_ref, g_ref, o_ref):
    o_ref[...] = _rms(x_ref[...], g_ref[...])


def _final_norm(x, g_row, row0, n_rows):
    d = x.shape[1]
    tm = ROW_TILE
    return pl.pallas_call(
        _final_norm_kernel,
        out_shape=jax.ShapeDtypeStruct((n_rows, d), F32),
        grid=(n_rows // tm,),
        in_specs=[pl.BlockSpec((tm, d), lambda i: (row0 // tm + i, 0)), pl.BlockSpec((1, d), lambda i: (0, 0))],
        out_specs=pl.BlockSpec((tm, d), lambda i: (i, 0)),
        compiler_params=_cparams("arbitrary"),
        name="final_norm",
    )(x, g_row)


def _mm_kernel(*refs, n_a, resid):
    a_refs = refs[:n_a]
    w_ref = refs[n_a]
    pos = n_a + 1
    if resid:
        x_ref, g_ref = refs[pos], refs[pos + 1]
        pos += 2
    o_ref, wb_ref = refs[pos], refs[pos + 1]

    @pl.when(pl.program_id(1) == 0)
    def _():
        wb_ref[...] = w_ref[...].astype(BF16)

    acc = None
    k0 = 0
    for a_ref in a_refs:
        ka = a_ref.shape[1]
        part = jnp.dot(a_ref[...], wb_ref[k0:k0 + ka, :], preferred_element_type=F32)
        acc = part if acc is None else acc + part
        k0 += ka
    if resid:
        o_ref[...] = x_ref[...] + g_ref[...] * acc
    else:
        o_ref[...] = acc.astype(o_ref.dtype)


def _matmul(a_list, w, w_layer, out_dtype, resid=None):
    m = a_list[0].shape[0]
    _, k, n = w.shape
    tm = MM_TM
    tn = min(MM_TN_RESID if resid is not None else MM_TN, n)
    in_specs = [pl.BlockSpec((tm, a.shape[1]), lambda j, i: (i, 0)) for a in a_list]
    in_specs.append(pl.BlockSpec((None, k, tn), lambda j, i: (w_layer, 0, j)))
    args = list(a_list) + [w]
    if resid is not None:
        x, mods3, layer, which, lay = resid
        in_specs.append(pl.BlockSpec((tm, tn), lambda j, i: (i, j)))
        in_specs.append(pl.BlockSpec(
            (None, 1, tn), lambda j, i: (_mod_index(layer, which, lay.mod_row(i, tm)), 0, j)))
        args += [x, mods3]
    return pl.pallas_call(
        functools.partial(_mm_kernel, n_a=len(a_list), resid=resid is not None),
        out_shape=jax.ShapeDtypeStruct((m, n), out_dtype),
        grid=(n // tn, m // tm),
        in_specs=in_specs,
        out_specs=pl.BlockSpec((tm, tn), lambda j, i: (i, j)),
        scratch_shapes=[pltpu.VMEM((k, tn), BF16)],
        compiler_params=_cparams("arbitrary", "arbitrary"),
        name="matmul_resid" if resid is not None else "matmul",
    )(*args)


def _cmul(ar, ai, br, bi):
    return ar * br - ai * bi, ar * bi + ai * br


def _s5_matrices(lam_re, lam_im, log_dt, b_re, b_im, c_re, c_im):
    s = S5_CHUNK
    f = lambda z: z.astype(F32)
    lam_re, lam_im, log_dt, b_re, b_im, c_re, c_im = map(f, (lam_re, lam_im, log_dt, b_re, b_im, c_re, c_im))
    dt = jnp.exp(log_dt)[..., None]
    kk = jnp.arange(s + 1, dtype=F32)[:, None, None, None]
    mag = jnp.exp(kk * (lam_re * dt)[None])
    ph = kk * (lam_im * dt)[None]
    pw_re, pw_im = mag * jnp.cos(ph), mag * jnp.sin(ph)
    a_re, a_im = pw_re[1], pw_im[1]
    den = lam_re * lam_re + lam_im * lam_im
    q_re, q_im = _cmul(a_re - 1.0, a_im, lam_re / den, -lam_im / den)
    bb_re, bb_im = _cmul(q_re[..., None], q_im[..., None], b_re, b_im)
    cp_re, cp_im = _cmul(c_re[:, :, None], c_im[:, :, None],
                         jnp.moveaxis(pw_re, 0, 2)[:, :, :, None, :], jnp.moveaxis(pw_im, 0, 2)[:, :, :, None, :])
    bt_re = jnp.swapaxes(bb_re, 2, 3)[:, :, None, None]
    bt_im = jnp.swapaxes(bb_im, 2, 3)[:, :, None, None]
    kern = jnp.sum(cp_re[:, :, :, :, None, :] * bt_re - cp_im[:, :, :, :, None, :] * bt_im, axis=-1)
    sp = np.arange(s)[:, None]
    so = np.arange(s)[None, :]
    lag_f = np.clip(so - sp, 0, s)
    lag_b = np.clip(sp - so, 0, s)
    kf = jnp.where(jnp.asarray(so >= sp)[None, :, :, None, None], kern[0][:, lag_f], 0.0)
    kb = jnp.where(jnp.asarray(sp >= so)[None, :, :, None, None], kern[1][:, lag_b], 0.0)
    g = kern.shape[1]
    n = s * S5_GROUP
    tmat = jnp.transpose(kf + kb, (0, 1, 4, 2, 3)).reshape(g, n, n)

    def e_mat(d, powers):
        pr = jnp.moveaxis(pw_re[powers, d], 0, 1)[:, :, :, None]
        pi = jnp.moveaxis(pw_im[powers, d], 0, 1)[:, :, :, None]
        er, ei = _cmul(pr, pi, bb_re[d][:, None], bb_im[d][:, None])
        er = jnp.transpose(er, (0, 1, 3, 2)).reshape(g, n, -1)
        ei = jnp.transpose(ei, (0, 1, 3, 2)).reshape(g, n, -1)
        return jnp.concatenate([er, ei], axis=-1)

    def c_mat(d, powers):
        cr = jnp.transpose(cp_re[d][:, powers], (0, 3, 1, 2)).reshape(g, -1, n)
        ci = jnp.transpose(cp_im[d][:, powers], (0, 3, 1, 2)).reshape(g, -1, n)
        return jnp.concatenate([cr, -ci], axis=1)

    swap = lambda e: jnp.concatenate([e[..., e.shape[-1] // 2:], e[..., :e.shape[-1] // 2]], axis=-1)
    e_f = e_mat(0, np.arange(s - 1, -1, -1))
    e_b = e_mat(1, np.arange(s))
    e_all = jnp.concatenate([e_f, swap(e_f), e_b, swap(e_b)], axis=-1)
    c_all = jnp.concatenate([c_mat(0, np.arange(1, s + 1)), c_mat(1, np.arange(s, 0, -1))], axis=1)
    dec_r = jnp.concatenate([pw_re[s], pw_re[s]], axis=-1)
    dec_i = jnp.concatenate([-pw_im[s], pw_im[s]], axis=-1)
    decay = jnp.stack([dec_r, dec_i], axis=2)[:, :, :, None, :]
    return tmat.astype(BF16), e_all.astype(BF16), c_all.astype(BF16), decay


def _block_transpose(v):
    n, w = v.shape
    rows = lax.broadcasted_iota(jnp.int32, v.shape, 0)
    lanes = lax.broadcasted_iota(jnp.int32, v.shape, 1)
    k = S5_CHUNK // 2
    while k >= 1:
        up = pltpu.roll(pltpu.roll(v, n - k, 0), S5_GROUP * k, 1)
        dn = pltpu.roll(pltpu.roll(v, k, 0), w - S5_GROUP * k, 1)
        rbit = (rows & k) != 0
        gbit = (lanes & (S5_GROUP * k)) != 0
        v = jnp.where(gbit, jnp.where(rbit, v, up), jnp.where(rbit, dn, v))
        k //= 2
    return v


def _s5_kernel(x_ref, t_ref, e_ref, c_ref, dec_ref, h0_ref, *rest, nbk, nc):
    n_state_refs = 6 * S5_GROUPS_IN_FLIGHT
    y_ref, hfin_ref, v0, v1, w0, w1 = rest[-(6 + n_state_refs):-n_state_refs]
    state_refs = rest[-n_state_refs:]
    lw = x_ref.shape[1]
    gpt = lw // S5_GROUP
    nr = nbk * nc
    p2 = state_refs[0].shape[1]
    n_blk = x_ref.shape[0] // S5_TRANSPOSE_ROWS

    def to_groups(i, carry):
        rows = pl.ds(pl.multiple_of(i * S5_TRANSPOSE_ROWS, S5_TRANSPOSE_ROWS), S5_TRANSPOSE_ROWS)
        v = _block_transpose(x_ref[rows, :])
        v0[rows, :] = v[:, :LANES]
        v1[rows, :] = v[:, LANES:]
        return carry

    lax.fori_loop(0, n_blk, to_groups, 0, unroll=S5_TRANSPOSE_UNROLL)

    def group(g, zf, zfs, zb, zbs, hsf, hsb):
        rows_g = pl.ds(g, nr, stride=gpt)
        u = jnp.concatenate([v0[rows_g, :], v1[rows_g, :]], axis=1).astype(BF16)
        z = jnp.dot(u, e_ref[g], preferred_element_type=F32)
        zf[...] = z[:, 0:p2]
        zfs[...] = z[:, p2:2 * p2]
        zb[...] = z[:, 2 * p2:3 * p2]
        zbs[...] = z[:, 3 * p2:4 * p2]
        ar_f, ai_f, ar_b, ai_b = dec_ref[0, g, 0], dec_ref[0, g, 1], dec_ref[1, g, 0], dec_ref[1, g, 1]
        hf, hfs, hb, hbs = h0_ref[g, 0], h0_ref[g, 1], h0_ref[g, 2], h0_ref[g, 3]
        for c in range(nc):
            rf = pl.ds(c, nbk, stride=nc)
            rb = pl.ds(nc - 1 - c, nbk, stride=nc)
            hsf[rf, :] = hf
            hsb[rb, :] = hb
            hf, hfs = ar_f * hf + ai_f * hfs + zf[rf, :], ar_f * hfs - ai_f * hf + zfs[rf, :]
            hb, hbs = ar_b * hb + ai_b * hbs + zb[rb, :], ar_b * hbs - ai_b * hb + zbs[rb, :]
        hfin_ref[g, 0] = hf
        hfin_ref[g, 1] = hb
        hs = jnp.concatenate([hsf[...], hsb[...]], axis=1).astype(BF16)
        y = jnp.dot(u, t_ref[g], preferred_element_type=F32)
        y = y + jnp.dot(hs, c_ref[g], preferred_element_type=F32)
        w0[rows_g, :] = y[:, :LANES]
        w1[rows_g, :] = y[:, LANES:]

    def groups(i, carry):
        for k in range(S5_GROUPS_IN_FLIGHT):
            group(i * S5_GROUPS_IN_FLIGHT + k, *state_refs[6 * k:6 * k + 6])
        return carry

    lax.fori_loop(0, gpt // S5_GROUPS_IN_FLIGHT, groups, 0)

    def to_tokens(i, carry):
        rows = pl.ds(pl.multiple_of(i * S5_TRANSPOSE_ROWS, S5_TRANSPOSE_ROWS), S5_TRANSPOSE_ROWS)
        y_ref[rows, :] = _block_transpose(jnp.concatenate([w0[rows, :], w1[rows, :]], axis=1))
        return carry

    lax.fori_loop(0, n_blk, to_tokens, 0, unroll=S5_TRANSPOSE_UNROLL)


def _s5_scan(proj, row0, h0, mats, mat_layer, n_seq, seq_len, width, y_prev):
    tmat, e_all, c_all, decay = mats
    n = tmat.shape[2]
    p2 = decay.shape[-1]
    lw = S5_LANE_TILE
    gpt = lw // S5_GROUP
    nc = seq_len // S5_CHUNK
    nbk = max(1, S5_ROW_BLOCK // seq_len)
    rb_rows = nbk * seq_len
    n_rb = n_seq // nbk
    nr = nbk * nc
    rb0 = row0 // rb_rows
    assert n_seq % nbk == 0 and row0 % rb_rows == 0 and p2 == LANES and lw == 2 * LANES
    g = tmat.shape[1]
    h0s = jnp.concatenate([h0[..., p2 // 2:], h0[..., :p2 // 2]], axis=-1)
    h04 = jnp.stack([h0[:, 0], h0s[:, 0], h0[:, 1], h0s[:, 1]], axis=1)
    h0p = jnp.transpose(h04.reshape(n_rb, nbk, 4, g, p2), (0, 3, 2, 1, 4))
    wspec = lambda r, c: pl.BlockSpec((None, gpt, r, c), lambda lt, rb: (mat_layer, lt, 0, 0))
    in_specs = [
        pl.BlockSpec((rb_rows, lw), lambda lt, rb: (rb0 + rb, lt)),
        wspec(n, n), wspec(n, 4 * p2), wspec(2 * p2, n),
        pl.BlockSpec((None, 2, gpt, 2, 1, p2), lambda lt, rb: (mat_layer, 0, lt, 0, 0, 0)),
        pl.BlockSpec((None, gpt, 4, nbk, p2), lambda lt, rb: (rb, lt, 0, 0, 0)),
    ]
    args = [proj, tmat, e_all, c_all, decay, h0p]
    aliases = {}
    if y_prev is not None:
        in_specs.append(pl.BlockSpec(memory_space=pl.ANY))
        args.append(y_prev)
        aliases = {6: 0}
    y, hfin = pl.pallas_call(
        functools.partial(_s5_kernel, nbk=nbk, nc=nc),
        out_shape=(jax.ShapeDtypeStruct((proj.shape[0], width), F32),
                   jax.ShapeDtypeStruct((n_rb, g, 2, nbk, p2), F32)),
        grid=(width // lw, n_rb),
        in_specs=in_specs,
        out_specs=(pl.BlockSpec((rb_rows, lw), lambda lt, rb: (rb0 + rb, lt)),
                   pl.BlockSpec((None, gpt, 2, nbk, p2), lambda lt, rb: (rb, lt, 0, 0, 0))),
        scratch_shapes=([pltpu.VMEM((rb_rows, LANES), F32)] * 4
                        + [pltpu.VMEM((nr, p2), F32)] * (6 * S5_GROUPS_IN_FLIGHT)),
        input_output_aliases=aliases,
        compiler_params=_cparams("arbitrary", "arbitrary"),
        name="s5_scan",
    )(*args)
    fin = jnp.transpose(hfin, (0, 3, 2, 1, 4)).reshape(n_seq, 2, g, p2)
    return y, fin


def _s5_glu_kernel(y_ref, u_ref, d_ref, w_ref, b_ref, o_ref, wb_ref):
    @pl.when(pl.program_id(0) == 0)
    def _():
        wb_ref[...] = w_ref[...].astype(BF16)

    z = _gelu(y_ref[...] + d_ref[...] * u_ref[...])
    gate = jnp.dot(z.astype(BF16), wb_ref[...], preferred_element_type=F32) + b_ref[...]
    o_ref[...] = (z * _sigmoid(gate)).astype(o_ref.dtype)


def _s5_glu(y, proj, d_row, w_glu, w_layer, b_row):
    t, w = y.shape
    tm = MM_TM
    return pl.pallas_call(
        _s5_glu_kernel,
        out_shape=jax.ShapeDtypeStruct((t, w), BF16),
        grid=(t // tm,),
        in_specs=[
            pl.BlockSpec((tm, w), lambda i: (i, 0)),
            pl.BlockSpec((tm, w), lambda i: (i, 0)),
            pl.BlockSpec((1, w), lambda i: (0, 0)),
            pl.BlockSpec((None, w, w), lambda i: (w_layer, 0, 0)),
            pl.BlockSpec((1, w), lambda i: (0, 0)),
        ],
        out_specs=pl.BlockSpec((tm, w), lambda i: (i, 0)),
        scratch_shapes=[pltpu.VMEM((w, w), BF16)],
        compiler_params=_cparams("arbitrary"),
        name="s5_glu",
    )(y, proj, d_row, w_glu, b_row)


def _lru_kernel(gate_ref, xr_ref, cw_ref, cb_ref, wa_ref, wx_ref, ba_ref, bx_ref, sp_ref, h0_ref, *rest):
    y_ref, hfin_ref, a_f, b_f, a_b, b_b, h_f, h_b = rest[-8:]
    seq, lw = xr_ref.shape
    x = xr_ref[...]
    rows = lax.broadcasted_iota(jnp.int32, (seq, lw), 0)
    cw = cw_ref[...]
    xc = cw[2:3] * x + cb_ref[...]
    xc = xc + cw[0:1] * jnp.where(rows >= 2, pltpu.roll(x, 2, 0), 0.0)
    xc = xc + cw[1:2] * jnp.where(rows >= 1, pltpu.roll(x, 1, 0), 0.0)
    xc = xc + cw[3:4] * jnp.where(rows < seq - 1, pltpu.roll(x, seq - 1, 0), 0.0)
    for hb in range(lw // LRU_HEAD_BLOCK):
        lanes = slice(hb * LRU_HEAD_BLOCK, (hb + 1) * LRU_HEAD_BLOCK)
        xb = xc[:, lanes]
        xbb = xb.astype(BF16)
        for d, (a_s, b_s) in enumerate(((a_f, b_f), (a_b, b_b))):
            r = _sigmoid(jnp.dot(xbb, wa_ref[d, hb], preferred_element_type=F32) + ba_ref[d:d + 1, lanes])
            gi = _sigmoid(jnp.dot(xbb, wx_ref[d, hb], preferred_element_type=F32) + bx_ref[d:d + 1, lanes])
            a = jnp.exp(-LRU_C * r * sp_ref[d:d + 1, lanes])
            a_s[:, lanes] = a
            b_s[:, lanes] = jnp.sqrt(1.0 - a * a) * (gi * xb)

    def step(t, carry):
        hf, hb = carry
        rf = pl.ds(t, 1)
        hf = a_f[rf, :] * hf + b_f[rf, :]
        h_f[rf, :] = hf
        rb = pl.ds(seq - 1 - t, 1)
        hb = a_b[rb, :] * hb + b_b[rb, :]
        h_b[rb, :] = hb
        return hf, hb

    hf, hb = lax.fori_loop(0, seq, step, (h0_ref[0:1, :], h0_ref[1:2, :]), unroll=8)
    hfin_ref[0:1, :] = hf
    hfin_ref[1:2, :] = hb
    y_ref[...] = (_gelu(gate_ref[...]) * (h_f[...] + h_b[...])).astype(y_ref.dtype)


def _lru_mixer(proj, row0, n_seq, seq_len, lane_w, h0, conv_w, conv_b, wa_bd, wx_bd, b_a, b_x, sp, y_prev):
    w = conv_w.shape[1]
    nlb = w // lane_w
    rb0 = row0 // seq_len
    hpb = lane_w // LRU_HEAD_BLOCK
    extra_specs, extra_args, aliases = [], [], {}
    if y_prev is not None:
        extra_specs, extra_args, aliases = [pl.BlockSpec(memory_space=pl.ANY)], [y_prev], {10: 0}
    return pl.pallas_call(
        _lru_kernel,
        out_shape=(jax.ShapeDtypeStruct((proj.shape[0], w), BF16), jax.ShapeDtypeStruct((n_seq, 2, w), F32)),
        grid=(n_seq, nlb),
        input_output_aliases=aliases,
        in_specs=[
            pl.BlockSpec((seq_len, lane_w), lambda b, c: (rb0 + b, nlb + c)),
            pl.BlockSpec((seq_len, lane_w), lambda b, c: (rb0 + b, 2 * nlb + c)),
            pl.BlockSpec((conv_w.shape[0], lane_w), lambda b, c: (0, c)),
            pl.BlockSpec((1, lane_w), lambda b, c: (0, c)),
            pl.BlockSpec((2, hpb, LRU_HEAD_BLOCK, LRU_HEAD_BLOCK), lambda b, c: (0, c, 0, 0)),
            pl.BlockSpec((2, hpb, LRU_HEAD_BLOCK, LRU_HEAD_BLOCK), lambda b, c: (0, c, 0, 0)),
            pl.BlockSpec((2, lane_w), lambda b, c: (0, c)),
            pl.BlockSpec((2, lane_w), lambda b, c: (0, c)),
            pl.BlockSpec((2, lane_w), lambda b, c: (0, c)),
            pl.BlockSpec((None, 2, lane_w), lambda b, c: (b, 0, c)),
        ] + extra_specs,
        out_specs=(pl.BlockSpec((seq_len, lane_w), lambda b, c: (rb0 + b, c)),
                   pl.BlockSpec((None, 2, lane_w), lambda b, c: (b, 0, c))),
        scratch_shapes=[pltpu.VMEM((seq_len, lane_w), F32)] * 6,
        compiler_params=_cparams("arbitrary", "arbitrary"),
        name="rglru",
    )(proj, proj, conv_w, conv_b, wa_bd, wx_bd, b_a, b_x, sp, h0, *extra_args)


def _block_diag_heads(w):
    two, h, hd, _ = w.shape
    per = LRU_HEAD_BLOCK // hd
    wb = w.reshape(two, h // per, per, hd, hd)
    eye = jnp.eye(per, dtype=w.dtype)
    bd = jnp.einsum('dbkij,kl->dbkilj', wb, eye)
    return bd.reshape(two, h // per, LRU_HEAD_BLOCK, LRU_HEAD_BLOCK).astype(BF16)


def _softmax_pv(parts):
    m = None
    for s, _ in parts:
        mm = jnp.max(s, axis=-1, keepdims=True)
        m = mm if m is None else jnp.maximum(m, mm)
    acc, den = None, None
    for s, v in parts:
        p = jnp.exp(s - m)
        l = jnp.sum(p, axis=-1, keepdims=True)
        o = jnp.dot(p.astype(BF16), v, preferred_element_type=F32)
        acc = o if acc is None else acc + o
        den = l if den is None else den + l
    return acc / den


def _qk(q, k):
    return lax.dot_general(q, k, (((1,), (1,)), ((), ())), preferred_element_type=F32)


def _ctx_attn_kernel(q_ref, k_ref, v_ref, *rest, n_heads):
    o_ref, ck_ref, cv_ref = rest[-3:]
    dh = q_ref.shape[1] // n_heads
    scale = dh ** -0.5
    ck_ref[...] = k_ref[...]
    cv_ref[...] = v_ref[...]
    for h in range(n_heads):
        lanes = slice(h * dh, (h + 1) * dh)
        q = q_ref[:, lanes].astype(BF16)
        k = k_ref[:, lanes].astype(BF16)
        v = v_ref[:, lanes].astype(BF16)
        o_ref[:, lanes] = _softmax_pv([(_qk(q, k) * scale, v)]).astype(o_ref.dtype)


def _ctx_attention(qkv, n_seq, seq_len, n_heads, total_rows, layer_j, n_attn_layers, caches):
    d = qkv.shape[1] // 3
    cache_shape = jax.ShapeDtypeStruct((n_seq, n_attn_layers, seq_len, d), F32)
    cache_spec = pl.BlockSpec((None, None, seq_len, d), lambda b: (b, layer_j, 0, 0))
    in_specs = [pl.BlockSpec((seq_len, d), lambda b, cb=cb: (b, cb)) for cb in range(3)]
    args = [qkv, qkv, qkv]
    aliases = {}
    if caches is not None:
        in_specs += [pl.BlockSpec(memory_space=pl.ANY)] * 2
        args += list(caches)
        aliases = {3: 1, 4: 2}
    return pl.pallas_call(
        functools.partial(_ctx_attn_kernel, n_heads=n_heads),
        out_shape=(jax.ShapeDtypeStruct((total_rows, d), BF16), cache_shape, cache_shape),
        grid=(n_seq,),
        in_specs=in_specs,
        out_specs=(pl.BlockSpec((seq_len, d), lambda b: (b, 0)), cache_spec, cache_spec),
        input_output_aliases=aliases,
        compiler_params=_cparams("arbitrary"),
        name="ctx_attention",
    )(*args)


def _nbr_row_windows(rows):
    kr = min(WIN_ROWS_MAX, rows)
    starts = np.clip(np.arange(rows) - kr // 2, 0, rows - kr)
    groups, r = [], 0
    while r < rows:
        r1 = r
        while r1 < rows and starts[r1] == starts[r]:
            r1 += 1
        groups.append((r, r1, int(starts[r])))
        r = r1
    return kr, starts, groups


def _nbr_attn_kernel(q_ref, k_ref, v_ref, kc_ref, vc_ref, bias_ref, o_in_ref, o_ref):
    del o_in_ref
    dh = q_ref.shape[1]
    scale = dh ** -0.5
    kr, starts, groups = _nbr_row_windows(q_ref.shape[0] // GRID_W)
    k_all = k_ref[...].astype(BF16)
    v_all = v_ref[...].astype(BF16)
    kc = kc_ref[...].astype(BF16)
    vc = vc_ref[...].astype(BF16)
    for r0, r1, rs in groups:
        q_rows = slice(r0 * GRID_W, r1 * GRID_W)
        k_rows = slice(rs * GRID_W, (rs + kr) * GRID_W)
        q = q_ref[q_rows, :].astype(BF16)
        bias = jnp.concatenate([bias_ref[int(starts[r]) - r + WIN_ROWS_MAX - 1] for r in range(r0, r1)], axis=0)
        s_loc = _qk(q, k_all[k_rows]) * scale + bias
        s_ctx = _qk(q, kc) * scale
        o_ref[q_rows, :] = _softmax_pv([(s_loc, v_all[k_rows]), (s_ctx, vc)]).astype(o_ref.dtype)


def _nbr_attention(qkv, o_all, cache_k, cache_v, layer_j, bias, row0, n_seq, seq_len, n_heads):
    d = qkv.shape[1] // 3
    dh = d // n_heads
    rb0 = row0 // seq_len
    past = cache_k.shape[2]
    ck = cache_k.reshape(cache_k.shape[0], cache_k.shape[1], past, d)
    cv = cache_v.reshape(ck.shape)
    cache_spec = pl.BlockSpec((None, None, past, dh), lambda h, b: (b, layer_j, 0, h))
    return pl.pallas_call(
        _nbr_attn_kernel,
        out_shape=jax.ShapeDtypeStruct(o_all.shape, o_all.dtype),
        grid=(n_heads, n_seq),
        in_specs=[
            pl.BlockSpec((seq_len, dh), lambda h, b: (rb0 + b, h)),
            pl.BlockSpec((seq_len, dh), lambda h, b: (rb0 + b, n_heads + h)),
            pl.BlockSpec((seq_len, dh), lambda h, b: (rb0 + b, 2 * n_heads + h)),
            cache_spec, cache_spec,
            pl.BlockSpec((None, None) + bias.shape[2:], lambda h, b: (layer_j, h, 0, 0, 0)),
            pl.BlockSpec(memory_space=pl.ANY),
        ],
        out_specs=pl.BlockSpec((seq_len, dh), lambda h, b: (rb0 + b, h)),
        input_output_aliases={6: 0},
        compiler_params=_cparams("arbitrary", "arbitrary"),
        name="nbr_attention",
    )(qkv, qkv, qkv, ck, cv, bias, o_all)


def _nbr_bias(rpb, rows):
    kr, _, _ = _nbr_row_windows(rows)
    c_idx = np.arange(GRID_W)
    col_start = np.clip(c_idx - WIN_COLS // 2, 0, GRID_W - WIN_COLS)
    kcol = np.arange(GRID_W)[None, :]
    col_valid = (kcol >= col_start[:, None]) & (kcol < col_start[:, None] + WIN_COLS)
    col_off = np.clip(kcol - c_idx[:, None] + WIN_COLS - 1, 0, 2 * WIN_COLS - 2)
    col_sel = jnp.asarray(np.eye(2 * WIN_COLS - 1, dtype=np.float32)[col_off])
    tab = jnp.einsum('hij,cmj->hcim', rpb.astype(F32), col_sel, precision=lax.Precision.HIGHEST)
    tab = jnp.where(jnp.asarray(col_valid)[None, :, None, :], tab, NEG_INF)
    n_win = 2 * WIN_ROWS_MAX - kr
    wins = [tab[:, :, i0:i0 + kr, :].reshape(tab.shape[0], GRID_W, kr * GRID_W) for i0 in range(n_win)]
    return jnp.stack(wins, axis=1)


def _row_copy(hbm, row, buf, slot, r, sem):
    return pltpu.make_async_copy(hbm.at[pl.ds(row, 1), :], buf.at[slot, pl.ds(r, 1), :], sem.at[slot])


def _moe_gather_kernel(src_ref, nv_ref, h_hbm, o_ref, buf, sem):
    i = pl.program_id(0)
    tm = o_ref.shape[0]
    slot = i % 2

    def issue_tile(tile, s):
        base = tile * tm

        def issue(q, c):
            for par in range(N_DMA_PRIORITIES):
                r = q * N_DMA_PRIORITIES + par
                _row_copy(h_hbm, src_ref[base + r], buf, s, r, sem).start(priority=par)
            return c

        lax.fori_loop(0, tm // N_DMA_PRIORITIES, issue, 0, unroll=4)

    @pl.when(i == 0)
    def _():
        issue_tile(0, 0)

    @pl.when(i + 1 < nv_ref[0])
    def _():
        issue_tile(i + 1, 1 - slot)

    @pl.when(i < nv_ref[0])
    def _():
        pltpu.make_async_copy(h_hbm.at[pl.ds(0, tm), :], buf.at[slot], sem.at[slot]).wait()
        o_ref[...] = buf[slot].astype(o_ref.dtype)

    @pl.when(i >= nv_ref[0])
    def _():
        o_ref[...] = jnp.zeros_like(o_ref)


def _moe_gather(h, src, n_valid, n_rows):
    tm = MOE_TM
    d = h.shape[1]
    return pl.pallas_call(
        _moe_gather_kernel,
        out_shape=jax.ShapeDtypeStruct((n_rows, d), BF16),
        grid_spec=pltpu.PrefetchScalarGridSpec(
            num_scalar_prefetch=2,
            grid=(n_rows // tm,),
            in_specs=[pl.BlockSpec(memory_space=pl.ANY)],
            out_specs=pl.BlockSpec((tm, d), lambda i, src, nv: (i, 0)),
            scratch_shapes=[pltpu.VMEM((2, tm, d), F32), pltpu.SemaphoreType.DMA((2,))],
        ),
        compiler_params=_cparams("arbitrary"),
        name="moe_gather",
    )(src, n_valid, h)


def _new_expert(te_ref, i):
    return (i == 0) | (te_ref[i] != te_ref[jnp.maximum(i - 1, 0)])


def _moe_gu_kernel(te_ref, nv_ref, x_ref, wg_ref, wu_ref, o_ref, wgb, wub):
    i = pl.program_id(1)
    valid = i < nv_ref[0]

    @pl.when(valid & _new_expert(te_ref, i))
    def _():
        wgb[...] = wg_ref[...].astype(BF16)
        wub[...] = wu_ref[...].astype(BF16)

    @pl.when(valid)
    def _():
        x = x_ref[...]
        a = jnp.dot(x, wgb[...], preferred_element_type=F32)
        b = jnp.dot(x, wub[...], preferred_element_type=F32)
        o_ref[...] = (a * jax.nn.sigmoid(a) * b).astype(o_ref.dtype)

    @pl.when(jnp.logical_not(valid))
    def _():
        o_ref[...] = jnp.zeros_like(o_ref)


def _moe_gate_up(xs, w_gu, layer, tile_expert, n_valid):
    p, d = xs.shape
    de = w_gu.shape[3] // 2
    tm, tn = MOE_TM, min(MOE_TN, de)
    nj = de // tn
    row = lambda i, nv: jnp.minimum(i, nv[0] - 1)
    return pl.pallas_call(
        _moe_gu_kernel,
        out_shape=jax.ShapeDtypeStruct((p, de), BF16),
        grid_spec=pltpu.PrefetchScalarGridSpec(
            num_scalar_prefetch=2,
            grid=(nj, p // tm),
            in_specs=[
                pl.BlockSpec((tm, d), lambda j, i, te, nv: (row(i, nv), 0)),
                pl.BlockSpec((None, None, d, tn), lambda j, i, te, nv: (layer, te[i], 0, j)),
                pl.BlockSpec((None, None, d, tn), lambda j, i, te, nv: (layer, te[i], 0, nj + j)),
            ],
            out_specs=pl.BlockSpec((tm, tn), lambda j, i, te, nv: (i, j)),
            scratch_shapes=[pltpu.VMEM((d, tn), BF16), pltpu.VMEM((d, tn), BF16)],
        ),
        compiler_params=_cparams("arbitrary", "arbitrary"),
        name="moe_gate_up",
    )(tile_expert, n_valid, xs, w_gu, w_gu)


def _moe_dn_kernel(te_ref, nv_ref, h_ref, w_ref, o_ref, wb):
    i = pl.program_id(0)
    valid = i < nv_ref[0]

    @pl.when(valid & _new_expert(te_ref, i))
    def _():
        wb[...] = w_ref[...].astype(BF16)

    @pl.when(valid)
    def _():
        o_ref[...] = jnp.dot(h_ref[...], wb[...], preferred_element_type=F32)

    @pl.when(jnp.logical_not(valid))
    def _():
        o_ref[...] = jnp.zeros_like(o_ref)


def _moe_down(hmid, w_dn, layer, tile_expert, n_valid):
    p, de = hmid.shape
    d = w_dn.shape[3]
    tm = MOE_TM
    row = lambda i, nv: jnp.minimum(i, nv[0] - 1)
    return pl.pallas_call(
        _moe_dn_kernel,
        out_shape=jax.ShapeDtypeStruct((p, d), F32),
        grid_spec=pltpu.PrefetchScalarGridSpec(
            num_scalar_prefetch=2,
            grid=(p // tm,),
            in_specs=[
                pl.BlockSpec((tm, de), lambda i, te, nv: (row(i, nv), 0)),
                pl.BlockSpec((None, None, de, d), lambda i, te, nv: (layer, te[i], 0, 0)),
            ],
            out_specs=pl.BlockSpec((tm, d), lambda i, te, nv: (i, 0)),
            scratch_shapes=[pltpu.VMEM((de, d), BF16)],
        ),
        compiler_params=_cparams("arbitrary"),
        name="moe_down",
    )(tile_expert, n_valid, hmid, w_dn)


def _moe_combine_kernel(p1_ref, p2_ref, y_hbm, x_ref, w_ref, g_ref, *rest, with_norm):
    if with_norm:
        ng_ref, nsh_ref, nsc_ref, o_ref, h_ref, buf1, buf2, sem = rest
    else:
        o_ref, buf1, buf2, sem = rest
    i = pl.program_id(0)
    tm = o_ref.shape[0]
    slot = i % 2

    def issue_tile(tile, s):
        base = tile * tm

        def issue(r, c):
            _row_copy(y_hbm, p1_ref[base + r], buf1, s, r, sem).start(priority=0)
            _row_copy(y_hbm, p2_ref[base + r], buf2, s, r, sem).start(priority=1)
            return c

        lax.fori_loop(0, tm, issue, 0, unroll=8)

    @pl.when(i == 0)
    def _():
        issue_tile(0, 0)

    @pl.when(i + 1 < pl.num_programs(0))
    def _():
        issue_tile(i + 1, 1 - slot)

    pltpu.make_async_copy(y_hbm.at[pl.ds(0, tm), :], buf1.at[slot], sem.at[slot]).wait()
    pltpu.make_async_copy(y_hbm.at[pl.ds(0, tm), :], buf2.at[slot], sem.at[slot]).wait()
    w = w_ref[...]
    y = w[:, 0:1] * buf1[slot] + w[:, 1:2] * buf2[slot]
    x_new = x_ref[...] + g_ref[...] * y
    o_ref[...] = x_new
    if with_norm:
        h_ref[...] = (_rms(x_new, ng_ref[...]) * (1.0 + nsc_ref[...]) + nsh_ref[...]).astype(h_ref.dtype)


def _moe_combine(y_sorted, x, wts, p1, p2, mods3, layer, lay, next_norm_g):
    t, d = x.shape
    tm = ROW_TILE
    with_norm = next_norm_g is not None
    mod = lambda lyr, which: pl.BlockSpec(
        (None, 1, d), lambda i, p1, p2: (_mod_index(lyr, which, lay.mod_row(i, tm)), 0, 0))
    row_spec = pl.BlockSpec((tm, d), lambda i, p1, p2: (i, 0))
    in_specs = [pl.BlockSpec(memory_space=pl.ANY), row_spec,
                pl.BlockSpec((tm, wts.shape[1]), lambda i, p1, p2: (i, 0)), mod(layer, 5)]
    args = [p1, p2, y_sorted, x, wts, mods3]
    out_shape = jax.ShapeDtypeStruct((t, d), F32)
    out_specs = row_spec
    if with_norm:
        in_specs += [pl.BlockSpec((1, d), lambda i, p1, p2: (0, 0)), mod(layer + 1, 0), mod(layer + 1, 1)]
        args += [next_norm_g, mods3, mods3]
        out_shape = (out_shape, jax.ShapeDtypeStruct((t, d), BF16))
        out_specs = (row_spec, row_spec)
    return pl.pallas_call(
        functools.partial(_moe_combine_kernel, with_norm=with_norm),
        out_shape=out_shape,
        grid_spec=pltpu.PrefetchScalarGridSpec(
            num_scalar_prefetch=2,
            grid=(t // tm,),
            in_specs=in_specs,
            out_specs=out_specs,
            scratch_shapes=[pltpu.VMEM((2, tm, d), F32), pltpu.VMEM((2, tm, d), F32), pltpu.SemaphoreType.DMA((2,))],
        ),
        compiler_params=_cparams("arbitrary"),
        name="moe_combine",
    )(*args)


def _moe_plan(route, n_experts):
    t = route.shape[1]
    tm = MOE_TM
    eid = route[0:2].astype(jnp.int32).reshape(-1)
    onehot = (eid[:, None] == jnp.arange(n_experts, dtype=jnp.int32)[None, :]).astype(jnp.int32)
    csum = jnp.cumsum(onehot, axis=0)
    rank = jnp.take_along_axis(csum, eid[:, None], axis=1)[:, 0] - 1
    counts = csum[-1]
    padded = ((counts + tm - 1) // tm) * tm
    ends = jnp.cumsum(padded)
    offs = ends - padded
    pos = offs[eid] + rank
    n_rows = 2 * t + n_experts * tm
    tok = jnp.tile(jnp.arange(t, dtype=jnp.int32), 2)
    src = (jnp.arange(n_rows, dtype=jnp.int32) % t).at[pos].set(tok)
    n_tiles = n_rows // tm
    n_valid = (ends[-1] // tm).astype(jnp.int32)
    starts = jnp.arange(n_tiles, dtype=jnp.int32) * tm
    te = jnp.sum((starts[:, None] >= ends[None, :]).astype(jnp.int32), axis=1)
    te_last = jnp.sum((((n_valid - 1) * tm) >= ends).astype(jnp.int32))
    te = jnp.where(starts < ends[-1], te, te_last).astype(jnp.int32)
    wts = jnp.transpose(route[2:4])
    wts = jnp.pad(wts, ((0, 0), (0, 6)))
    return src, te, n_valid.reshape(1), pos[:t], pos[t:], wts, n_rows


def _moe_layer(x, h2, route, mods3, layer, lay, w_gu, w_dn, next_norm_g):
    n_experts = w_gu.shape[1]
    src, te, n_valid, p1, p2, wts, n_rows = _moe_plan(route, n_experts)
    xs = _moe_gather(h2, src, n_valid, n_rows)
    hmid = _moe_gate_up(xs, w_gu, layer, te, n_valid)
    ys = _moe_down(hmid, w_dn, layer, te, n_valid)
    return _moe_combine(ys, x, wts, p1, p2, mods3, layer, lay, next_norm_g)


def kernel(x_prompt, x_sample, state_s5_re, state_s5_im, state_lru, cache_attn_k, cache_attn_v, c, c_ctx, w_ada, b_ada, norm1_g, norm2_g, final_norm_g, w_in_even, w_out_even, s5_lam_re, s5_lam_im, s5_log_dt, s5_b_re, s5_b_im, s5_c_re, s5_c_im, s5_d, s5_w_glu, s5_b_glu, lru_conv_w, lru_conv_b, lru_w_a, lru_b_a, lru_w_x, lru_b_x, lru_lam, w_qkv, w_o, rpb, w_router, b_router, w_gate_up, w_down):
    batch, seq, d = x_prompt.shape
    dec_batch, dec_seq, _ = x_sample.shape
    depth = w_ada.shape[0]
    n_heads = cache_attn_k.shape[3]
    s5_w = s5_d.shape[1]
    lru_w = lru_conv_w.shape[2]
    n_groups, n_state = s5_lam_re.shape[2], s5_lam_re.shape[3]
    assert dec_batch < MOD_ROWS
    n_ctx = batch * seq
    assert n_ctx % MM_TM == 0 and dec_seq % MM_TM == 0 and seq % ROW_TILE == 0 and dec_seq % ROW_TILE == 0
    lay = _Layout(n_ctx, dec_batch, dec_seq)
    t = lay.total

    x = jnp.concatenate([x_prompt.reshape(n_ctx, d), x_sample.reshape(dec_batch * dec_seq, d)], axis=0)
    cvec = jnp.zeros((MOD_ROWS, d), F32).at[:dec_batch].set(c).at[dec_batch].set(c_ctx)
    mods = _ada_project(cvec, w_ada, b_ada)
    mods3 = mods.reshape(depth * MOD_ROWS * N_MOD, 1, d)
    w_router_t = jnp.transpose(w_router)
    b_router_col = b_router.reshape(-1, 1)
    s5_mats = jax.vmap(_s5_matrices)(s5_lam_re, s5_lam_im, s5_log_dt, s5_b_re, s5_b_im, s5_c_re, s5_c_im)
    nbr_bias = jax.vmap(lambda r: _nbr_bias(r, dec_seq // GRID_W))(rpb)

    s5_re_list, s5_im_list, lru_list = [], [], []
    kv_caches = None
    h1 = _norm_mod(x, norm1_g[0].reshape(1, d), mods3, 0, 0, lay, BF16)
    for l in range(depth):
        j = l // 2
        if l % 2 == 0:
            proj = _matmul([h1], w_in_even, j, F32)
            zero_h0 = jnp.zeros((batch, 2, n_groups, 2 * n_state), F32)
            lat_h0 = jnp.concatenate([state_s5_re[:, j], state_s5_im[:, j]], axis=-1).astype(F32)
            y_scan, s5_fin = _s5_scan(proj, 0, zero_h0, s5_mats, j, batch, seq, s5_w, None)
            y_scan, _ = _s5_scan(proj, n_ctx, lat_h0, s5_mats, j, dec_batch, dec_seq, s5_w, y_scan)
            s5_re_list.append(s5_fin[..., :n_state])
            s5_im_list.append(s5_fin[..., n_state:])
            y_s5 = _s5_glu(y_scan, proj, s5_d[j].reshape(1, s5_w), s5_w_glu, j, s5_b_glu[j].reshape(1, s5_w))
            wa_bd = _block_diag_heads(lru_w_a[j])
            wx_bd = _block_diag_heads(lru_w_x[j])
            sp = jax.nn.softplus(-lru_lam[j].astype(F32))
            lru_args = (lru_conv_w[j], lru_conv_b[j].reshape(1, lru_w), wa_bd, wx_bd, lru_b_a[j], lru_b_x[j], sp)
            y_lru, lru_fin = _lru_mixer(proj, 0, batch, seq, lru_w, jnp.zeros((batch, 2, lru_w), F32),
                                        *lru_args, None)
            y_lru, _ = _lru_mixer(proj, n_ctx, dec_batch, dec_seq, lru_w // 2, state_lru[:, j].astype(F32),
                                  *lru_args, y_lru)
            lru_list.append(lru_fin)
            x = _matmul([y_s5, y_lru], w_out_even, j, F32, resid=(x, mods3, l, 2, lay))
        else:
            qkv = _matmul([h1], w_qkv, j, F32)
            o_all, *kv_caches = _ctx_attention(qkv, batch, seq, n_heads, t, j, depth // 2, kv_caches)
            o_all = _nbr_attention(qkv, o_all, cache_attn_k, cache_attn_v, j, nbr_bias, n_ctx, dec_batch, dec_seq,
                                   n_heads)
            x = _matmul([o_all], w_o, j, F32, resid=(x, mods3, l, 2, lay))
        h2, route = _norm_mod_route(x, norm2_g[l].reshape(1, d), mods3, l, lay, w_router_t, b_router_col)
        if l + 1 < depth:
            x, h1 = _moe_layer(x, h2, route, mods3, l, lay, w_gate_up, w_down, norm1_g[l + 1].reshape(1, d))
        else:
            x = _moe_layer(x, h2, route, mods3, l, lay, w_gate_up, w_down, None)

    g_fin = final_norm_g.reshape(1, d)
    y_prompt = _final_norm(x, g_fin, 0, n_ctx).reshape(batch, seq, d)
    y_sample = _final_norm(x, g_fin, n_ctx, t - n_ctx).reshape(dec_batch, dec_seq, d)
    cache_shape = (batch, depth // 2, seq, n_heads, d // n_heads)
    return (y_prompt, y_sample, jnp.stack(s5_re_list, axis=1), jnp.stack(s5_im_list, axis=1),
            jnp.stack(lru_list, axis=1), kv_caches[0].reshape(cache_shape), kv_caches[1].reshape(cache_shape))
```
